```python
import math
import jax, jax.numpy as jnp
from jax import lax
import numpy as np

D_MODEL = 1024
BATCH = 8
SEQ = 4096
DEPTH = 4

GRID_W = 64
CTX_LEN = 256
N_MIXERS = 2
NA_HEADS = 16
NA_HEAD_DIM = D_MODEL // NA_HEADS
NA_KR = 8
NA_KC = 16
NA_QCB = 16
NA_BAND = NA_QCB + NA_KC
GDN_HEADS = 8
GDN_HEAD_DIM = D_MODEL // GDN_HEADS
GDN_CONV = 5
GDN_CHUNK = 64
N_EXPERTS = 16
EC_CAPACITY_FACTOR = 2
D_EXPERT = 2048
DEEPNORM_ALPHA = (2.0 * DEPTH) ** 0.25
DEEPNORM_BETA = (8.0 * DEPTH) ** -0.25
LN_EPS = 1e-6
NEG_INF = -1e30

kernel_name = "hybrid_natten_gdn_ecmoe_dit"


def _layernorm(x, g, b):
    xf = x.astype(jnp.float32)
    mu = jnp.mean(xf, -1, keepdims=True)
    var = jnp.mean(jnp.square(xf - mu), -1, keepdims=True)
    y = (xf - mu) * lax.rsqrt(var + LN_EPS) * g.astype(jnp.float32) + b.astype(jnp.float32)
    return y.astype(x.dtype)


def _modulate(x, shift, scale):
    return x * (1 + scale) + shift


def _natten_tables(rows):
    kr = min(NA_KR, rows)
    r = np.arange(rows)
    row_start = np.clip(r - kr // 2, 0, rows - kr)
    dr_idx = row_start[:, None] + np.arange(kr)[None, :] - r[:, None] + NA_KR - 1
    n_cb = GRID_W // NA_QCB
    band_start = np.clip(np.arange(n_cb) * NA_QCB - NA_KC // 2, 0, GRID_W - NA_BAND)
    band_cols = band_start[:, None] + np.arange(NA_BAND)[None, :]
    qcol = np.arange(GRID_W).reshape(n_cb, NA_QCB)
    col_start = np.clip(qcol - NA_KC // 2, 0, GRID_W - NA_KC)
    kc = band_cols[:, None, :]
    valid = (kc >= col_start[..., None]) & (kc < col_start[..., None] + NA_KC)
    dc_idx = np.clip(kc - qcol[..., None], -(NA_KC - 1), NA_KC - 1) + NA_KC - 1
    return (kr, row_start.astype(np.int32), dr_idx.astype(np.int32),
            band_cols.astype(np.int32), valid, dc_idx.astype(np.int32))


def _natten_mixer(h, hc, w_qkv, w_o, rpb, need_ctx):
    B, N, D = h.shape
    L = hc.shape[1]
    H, dh = NA_HEADS, NA_HEAD_DIM
    rows = N // GRID_W
    n_cb = GRID_W // NA_QCB
    kr, row_start, dr_idx, band_cols, valid, dc_idx = _natten_tables(rows)
    scale = dh ** -0.5
    qkv = (h @ w_qkv).reshape(B, rows, GRID_W, 3, H, dh)
    q, k, v = qkv[..., 0, :, :], qkv[..., 1, :, :], qkv[..., 2, :, :]
    qkvc = (hc @ w_qkv).reshape(B, L, 3, H, dh)
    qc, kc, vc = qkvc[:, :, 0], qkvc[:, :, 1], qkvc[:, :, 2]
    mask_add = jnp.where(jnp.asarray(valid), 0.0, NEG_INF).astype(jnp.float32)
    bias_tab = rpb.astype(jnp.float32)[:, :, dc_idx] + mask_add[None, None]
    q_rows = jnp.moveaxis(q, 1, 0).reshape(rows, B, n_cb, NA_QCB, H, dh)

    def row_block(args):
        q_r, rs, dr = args
        k_rows = lax.dynamic_slice_in_dim(k, rs, kr, axis=1)
        v_rows = lax.dynamic_slice_in_dim(v, rs, kr, axis=1)
        k_band = k_rows[:, :, band_cols]
        v_band = v_rows[:, :, band_cols]
        s_win = jnp.einsum('bjqhd,bijkhd->bhjqik', q_r, k_band).astype(jnp.float32) * scale
        bias = jnp.take(bias_tab, dr, axis=1).transpose(0, 2, 3, 1, 4)
        s_win = (s_win + bias[None]).reshape(B, H, n_cb, NA_QCB, kr * NA_BAND)
        s_ctx = jnp.einsum('bjqhd,blhd->bhjql', q_r, kc).astype(jnp.float32) * scale
        p = jax.nn.softmax(jnp.concatenate([s_win, s_ctx], -1), axis=-1).astype(v.dtype)
        p_win = p[..., :kr * NA_BAND].reshape(B, H, n_cb, NA_QCB, kr, NA_BAND)
        p_ctx = p[..., kr * NA_BAND:]
        o = (jnp.einsum('bhjqik,bijkhd->bjqhd', p_win, v_band)
             + jnp.einsum('bhjql,blhd->bjqhd', p_ctx, vc))
        return o.reshape(B, GRID_W, H * dh)

    o = lax.map(row_block, (q_rows, jnp.asarray(row_start), jnp.asarray(dr_idx)))
    y = jnp.moveaxis(o, 0, 1).reshape(B, N, D) @ w_o
    yc = None
    if need_ctx:
        sc = jnp.einsum('blhd,bmhd->bhlm', qc, kc).astype(jnp.float32) * scale
        pc = jax.nn.softmax(sc, axis=-1).astype(vc.dtype)
        yc = jnp.einsum('bhlm,bmhd->blhd', pc, vc).reshape(B, L, D) @ w_o
    return y, yc


def _short_conv(u, w):
    C = u.shape[-1]
    pad = GDN_CONV // 2
    return lax.conv_general_dilated(u, w[:, None, :], window_strides=(1,), padding=[(pad, pad)],
                                    dimension_numbers=('NWC', 'WIO', 'NWC'), feature_group_count=C)


def _l2norm(a):
    return a * lax.rsqrt(jnp.sum(a * a, -1, keepdims=True) + 1e-6)


def _gated_delta_chunked(q, k, v, beta, g, s0):
    B, T, H, dk = q.shape
    dv = v.shape[-1]
    C = GDN_CHUNK
    n = T // C
    chunks = lambda a: a.reshape(B, n, C, H, *a.shape[3:]).swapaxes(2, 3)
    q = chunks(q) * dk ** -0.5
    k, v, beta, g = chunks(k), chunks(v), chunks(beta), chunks(g)
    G = jnp.cumsum(g, axis=-1)
    incl = jnp.tril(jnp.ones((C, C), bool))
    strict = jnp.tril(jnp.ones((C, C), bool), -1)
    decay = jnp.exp(jnp.where(incl, G[..., :, None] - G[..., None, :], -jnp.inf))
    kb = k * beta[..., None]
    A = jnp.where(strict, jnp.einsum('bnhcd,bnhsd->bnhcs', kb, k) * decay, 0.0)
    rhs = jnp.concatenate([kb * jnp.exp(G)[..., None], v * beta[..., None]], -1)
    wu = lax.linalg.triangular_solve(A + jnp.eye(C, dtype=A.dtype), rhs, left_side=True,
                                     lower=True, unit_diagonal=True)
    w, u = wu[..., :dk], wu[..., dk:]
    attn = jnp.einsum('bnhcd,bnhsd->bnhcs', q, k) * decay
    q_dec = q * jnp.exp(G)[..., None]
    k_tail = k * jnp.exp(G[..., -1:] - G)[..., None]
    g_tot = jnp.exp(G[..., -1])
    xs = tuple(jnp.moveaxis(a, 1, 0) for a in (w, u, attn, q_dec, k_tail, g_tot))

    def step(S, inp):
        w_c, u_c, a_c, qd_c, kt_c, gt_c = inp
        v_new = u_c - jnp.einsum('bhcd,bhde->bhce', w_c, S)
        o_c = jnp.einsum('bhcd,bhde->bhce', qd_c, S) + jnp.einsum('bhcs,bhse->bhce', a_c, v_new)
        S = S * gt_c[..., None, None] + jnp.einsum('bhcd,bhce->bhde', kt_c, v_new)
        return S, o_c

    S, o = lax.scan(step, s0, xs)
    return o.transpose(1, 0, 3, 2, 4).reshape(B, T, H, dv), S


def _gdn_project(u, w_in, conv_w, a_log, dt_bias):
    B, T, D = u.shape
    H, dk = GDN_HEADS, GDN_HEAD_DIM
    p = u @ w_in
    qkv = jax.nn.silu(_short_conv(p[..., :3 * D], conv_w)).astype(jnp.float32)
    q = _l2norm(qkv[..., :D].reshape(B, T, H, dk))
    k = _l2norm(qkv[..., D:2 * D].reshape(B, T, H, dk))
    v = qkv[..., 2 * D:].reshape(B, T, H, dk)
    z = p[..., 3 * D:4 * D]
    ab = p[..., 4 * D:].astype(jnp.float32).reshape(B, T, 2, 2, H)
    g = -jnp.exp(a_log.astype(jnp.float32)) * jax.nn.softplus(ab[:, :, 0] + dt_bias.astype(jnp.float32))
    beta = jax.nn.sigmoid(ab[:, :, 1])
    return q, k, v, g, beta, z


def _gdn_out(o, z, norm_w):
    B, T, H, dv = o.shape
    zf = z.astype(jnp.float32).reshape(B, T, H, dv)
    y = o * lax.rsqrt(jnp.mean(o * o, -1, keepdims=True) + LN_EPS) * norm_w.astype(jnp.float32)
    return (y * jax.nn.silu(zf)).reshape(B, T, H * dv).astype(z.dtype)


def _gdn_mixer(h, hc, w_in, conv_w, a_log, dt_bias, norm_w, w_o, need_ctx):
    B = h.shape[0]
    q, k, v, g, beta, z = _gdn_project(h, w_in, conv_w, a_log, dt_bias)
    qc, kc, vc, gc, betac, zc = _gdn_project(hc, w_in, conv_w, a_log, dt_bias)
    s0 = jnp.zeros((B, GDN_HEADS, GDN_HEAD_DIM, GDN_HEAD_DIM), jnp.float32)
    rev = lambda a: jnp.flip(a, axis=1)
    oc_f, s_f = _gated_delta_chunked(qc, kc, vc, betac[:, :, 0], gc[:, :, 0], s0)
    o_f, _ = _gated_delta_chunked(q, k, v, beta[:, :, 0], g[:, :, 0], s_f)
    oc_b, s_b = _gated_delta_chunked(rev(qc), rev(kc), rev(vc), rev(betac[:, :, 1]), rev(gc[:, :, 1]), s0)
    o_b, _ = _gated_delta_chunked(rev(q), rev(k), rev(v), rev(beta[:, :, 1]), rev(g[:, :, 1]), s_b)
    y = _gdn_out(o_f + rev(o_b), z, norm_w) @ w_o
    yc = _gdn_out(oc_f + rev(oc_b), zc, norm_w) @ w_o if need_ctx else None
    return y, yc


def _ec_moe(h, w_router, w_gate, w_up, w_down):
    B, n, D = h.shape
    cap = EC_CAPACITY_FACTOR * n // N_EXPERTS
    aff = jax.nn.softmax((h @ w_router).astype(jnp.float32), axis=-1)
    gate, idx = lax.top_k(jnp.swapaxes(aff, 1, 2), cap)
    xs = jax.vmap(lambda hb, ib: hb[ib])(h, idx)
    a = jnp.einsum('becd,edf->becf', xs, w_gate)
    u = jnp.einsum('becd,edf->becf', xs, w_up)
    ye = jnp.einsum('becf,efd->becd', jax.nn.silu(a) * u, w_down) * gate[..., None].astype(h.dtype)
    return jax.vmap(lambda yb, ib: jnp.zeros((n, D), h.dtype).at[ib.reshape(-1)].add(yb.reshape(-1, D)))(ye, idx)


def setup_inputs(seed: int = 0) -> dict:
    key = jax.random.key(seed)
    ks = jax.random.split(key, 24)
    D, H = D_MODEL, GDN_HEADS
    nA, nB = (DEPTH + 1) // 2, DEPTH // 2
    f32 = jnp.float32
    nrm = lambda k, shape, std: jax.random.normal(k, shape, f32) * std
    dt = jnp.exp(jax.random.uniform(ks[13], (nB, 2, H), f32, math.log(1e-3), math.log(1e-1)))
    return {
        "x": nrm(ks[0], (BATCH, SEQ, D), 1.0),
        "c": nrm(ks[1], (BATCH, D), 1.0),
        "ctx": nrm(ks[2], (BATCH, CTX_LEN, D), 1.0),
        "c_ctx": nrm(ks[3], (D,), 1.0),
        "ada_w": nrm(ks[4], (DEPTH, D, 6 * D), D ** -0.5),
        "ada_b": nrm(ks[5], (DEPTH, 6 * D), 0.02),
        "ln_g": 1.0 + nrm(ks[6], (DEPTH, 2, D), 0.02),
        "ln_b": nrm(ks[7], (DEPTH, 2, D), 0.02),
        "na_w_qkv": nrm(ks[8], (nA, D, 3 * D), D ** -0.5),
        "na_w_o": nrm(ks[9], (nA, D, D), D ** -0.5 * DEEPNORM_BETA),
        "na_rpb": nrm(ks[10], (nA, NA_HEADS, 2 * NA_KR - 1, 2 * NA_KC - 1), 0.02),
        "gdn_w_in": nrm(ks[11], (nB, D, 4 * D + 4 * H), D ** -0.5),
        "gdn_conv_w": nrm(ks[12], (nB, GDN_CONV, 3 * D), GDN_CONV ** -0.5),
        "gdn_a_log": jnp.log(jax.random.uniform(ks[14], (nB, 2, H), f32, 1.0, 16.0)),
        "gdn_dt_bias": dt + jnp.log(-jnp.expm1(-dt)),
        "gdn_norm_w": 1.0 + nrm(ks[15], (nB, GDN_HEAD_DIM), 0.02),
        "gdn_w_o": nrm(ks[16], (nB, D, D), D ** -0.5 * DEEPNORM_BETA),
        "moe_w_router": nrm(ks[17], (DEPTH, D, N_EXPERTS), D ** -0.5),
        "moe_w_gate": nrm(ks[18], (DEPTH, N_EXPERTS, D, D_EXPERT), D ** -0.5),
        "moe_w_up": nrm(ks[19], (DEPTH, N_EXPERTS, D, D_EXPERT), D ** -0.5),
        "moe_w_down": nrm(ks[20], (DEPTH, N_EXPERTS, D_EXPERT, D), D_EXPERT ** -0.5 * DEEPNORM_BETA),
    }


def reference(x, c, ctx, c_ctx, ada_w, ada_b, ln_g, ln_b, na_w_qkv, na_w_o, na_rpb,
              gdn_w_in, gdn_conv_w, gdn_a_log, gdn_dt_bias, gdn_norm_w, gdn_w_o,
              moe_w_router, moe_w_gate, moe_w_up, moe_w_down):
    alpha = DEEPNORM_ALPHA
    xc = ctx
    silu_c = jax.nn.silu(c)[:, None, :]
    silu_cc = jax.nn.silu(c_ctx)[None, None, :]
    for l in range(DEPTH):
        last = l == DEPTH - 1
        i = l // N_MIXERS
        mod = jnp.split(silu_c @ ada_w[l] + ada_b[l], 6, axis=-1)
        modc = jnp.split(silu_cc @ ada_w[l] + ada_b[l], 6, axis=-1)
        h = _modulate(x, mod[0], mod[1])
        hc = _modulate(xc, modc[0], modc[1])
        if l % N_MIXERS == 0:
            y, yc = _natten_mixer(h, hc, na_w_qkv[i], na_w_o[i], na_rpb[i], not last)
        else:
            y, yc = _gdn_mixer(h, hc, gdn_w_in[i], gdn_conv_w[i], gdn_a_log[i], gdn_dt_bias[i],
                               gdn_norm_w[i], gdn_w_o[i], not last)
        x = _layernorm(alpha * x + mod[2] * y, ln_g[l, 0], ln_b[l, 0])
        h = _modulate(x, mod[3], mod[4])
        x = _layernorm(alpha * x + mod[5] * _ec_moe(h, moe_w_router[l], moe_w_gate[l], moe_w_up[l], moe_w_down[l]),
                       ln_g[l, 1], ln_b[l, 1])
        if not last:
            xc = _layernorm(alpha * xc + modc[2] * yc, ln_g[l, 0], ln_b[l, 0])
            hc = _modulate(xc, modc[3], modc[4])
            xc = _layernorm(alpha * xc + modc[5] * _ec_moe(hc, moe_w_router[l], moe_w_gate[l], moe_w_up[l], moe_w_down[l]),
                            ln_g[l, 1], ln_b[l, 1])
    return x
```

```python
import functools
import math

import numpy as np
import jax
import jax.numpy as jnp
from jax import lax
from jax.experimental import pallas as pl
from jax.experimental.pallas import tpu as pltpu

F32 = jnp.float32
BF16 = jnp.bfloat16
I32 = jnp.int32
HIGHEST = lax.Precision.HIGHEST

GRID_W = 64
NA_HEADS = 16
NA_HEAD_DIM = 64
NA_KR = 8
NA_KC = 16
GDN_HEADS = 8
GDN_HEAD_DIM = 128
GDN_CONV = 5
N_EXPERTS = 16
EC_CAPACITY_FACTOR = 2
LN_EPS = 1e-6
NEG_INF = -1e30

GDN_CHUNK = 128
CONV_HALO = 8
SUBLANES = 8
LANES = 128
V7X_VMEM_BYTES = 64 * 1024 * 1024
VMEM_BIG = 56 * 1024 * 1024
VMEM_MID = 40 * 1024 * 1024

_NT = (((1,), (1,)), ((), ()))
_TN = (((0,), (0,)), ((), ()))


def _params(n_axes, vmem):
    return pltpu.CompilerParams(dimension_semantics=("arbitrary",) * n_axes, vmem_limit_bytes=vmem)


def _silu(x):
    return x * jax.nn.sigmoid(x)


def _layernorm_rows(z, g, b):
    mu = jnp.mean(z, -1, keepdims=True)
    zc = z - mu
    var = jnp.mean(zc * zc, -1, keepdims=True)
    return zc * lax.rsqrt(var + LN_EPS) * g + b


def _ada_kernel(cs_ref, w_ref, b_ref, o_ref):
    s = _silu(cs_ref[...])
    o_ref[0] = jnp.dot(s, w_ref[0], preferred_element_type=F32, precision=HIGHEST) + b_ref[0]


def ada_modulation(cs, ada_w, ada_b):
    depth, D, D6 = ada_w.shape
    R = cs.shape[0]
    tn = D6 // 4
    return pl.pallas_call(
        _ada_kernel,
        grid=(depth, D6 // tn),
        in_specs=[pl.BlockSpec((R, D), lambda l, j: (0, 0)),
                  pl.BlockSpec((1, D, tn), lambda l, j: (l, 0, j)),
                  pl.BlockSpec((1, 1, tn), lambda l, j: (l, 0, j))],
        out_specs=pl.BlockSpec((1, R, tn), lambda l, j: (l, 0, j)),
        out_shape=jax.ShapeDtypeStruct((depth, R, D6), F32),
        compiler_params=_params(2, VMEM_MID),
        name="ada_modulation",
    )(cs, ada_w, ada_b.reshape(depth, 1, D6))


def _mm_mod_kernel(x_ref, sc_ref, sh_ref, w_ref, o_ref, wb_ref):
    @pl.when((pl.program_id(1) == 0) & (pl.program_id(2) == 0))
    def _():
        wb_ref[...] = w_ref[...].astype(BF16)

    h = x_ref[0] * (1.0 + sc_ref[0]) + sh_ref[0]
    o_ref[0] = jnp.dot(h.astype(BF16), wb_ref[...], preferred_element_type=F32).astype(o_ref.dtype)


def mm_mod(x, sc, sh, w, out_dtype, tm, tn):
    B, N, D = x.shape
    NO = w.shape[1]
    tm = min(tm, N)
    tn = min(tn, NO)
    return pl.pallas_call(
        _mm_mod_kernel,
        grid=(NO // tn, B, N // tm),
        in_specs=[pl.BlockSpec((1, tm, D), lambda j, b, i: (b, i, 0)),
                  pl.BlockSpec((1, 1, D), lambda j, b, i: (b, 0, 0)),
                  pl.BlockSpec((1, 1, D), lambda j, b, i: (b, 0, 0)),
                  pl.BlockSpec((D, tn), lambda j, b, i: (0, j))],
        out_specs=pl.BlockSpec((1, tm, tn), lambda j, b, i: (b, i, j)),
        out_shape=jax.ShapeDtypeStruct((B, N, NO), out_dtype),
        scratch_shapes=[pltpu.VMEM((D, tn), BF16)],
        compiler_params=_params(3, VMEM_MID),
        name="mm_mod",
    )(x, sc, sh, w)


def _natten_bias_table(rpb):
    H = rpb.shape[0]
    qc = np.arange(GRID_W)
    kc = np.arange(GRID_W)
    col_start = np.clip(qc - NA_KC // 2, 0, GRID_W - NA_KC)
    valid = (kc[None, :] >= col_start[:, None]) & (kc[None, :] < col_start[:, None] + NA_KC)
    dc = np.clip(kc[None, :] - qc[:, None], -(NA_KC - 1), NA_KC - 1) + NA_KC - 1
    tab = rpb.astype(F32)[:, :, dc] + jnp.where(jnp.asarray(valid), 0.0, NEG_INF).astype(F32)[None, None]
    variants = [tab[:, d0:d0 + NA_KR].transpose(0, 2, 1, 3).reshape(H, GRID_W, NA_KR * GRID_W)
                for d0 in range(NA_KR)]
    return jnp.stack(variants)


def _softmax_pv(s_list, v_list):
    m = functools.reduce(jnp.maximum, [jnp.max(s, -1, keepdims=True) for s in s_list])
    ps = [jnp.exp(s - m) for s in s_list]
    den = functools.reduce(lambda a, b: a + b, [jnp.sum(p, -1, keepdims=True) for p in ps])
    o = functools.reduce(lambda a, b: a + b,
                         [jnp.dot(p.astype(BF16), v, preferred_element_type=F32) for p, v in zip(ps, v_list)])
    return o / den


def _natten_kernel(q_ref, k_ref, v_ref, kc_ref, vc_ref, bias_ref, o_ref, *, rows, scale):
    r = pl.program_id(1)
    rs = jnp.clip(r - NA_KR // 2, 0, rows - NA_KR)
    start = pl.multiple_of(rs * GRID_W, GRID_W)
    win = pl.ds(start, NA_KR * GRID_W)
    for h in range(NA_HEADS):
        sl = slice(h * NA_HEAD_DIM, (h + 1) * NA_HEAD_DIM)
        qh = q_ref[0, :, sl]
        s_w = lax.dot_general(qh, k_ref[0, win, sl], _NT, preferred_element_type=F32) * scale + bias_ref[0, h]
        s_c = lax.dot_general(qh, kc_ref[0, :, sl], _NT, preferred_element_type=F32) * scale
        o = _softmax_pv([s_w, s_c], [v_ref[0, win, sl], vc_ref[0, :, sl]])
        o_ref[0, :, sl] = o.astype(o_ref.dtype)


def natten(qkv, qkvc, bias_tab):
    B, N, D3 = qkv.shape
    D = D3 // 3
    L = qkvc.shape[1]
    rows = N // GRID_W
    assert rows >= NA_KR and N % GRID_W == 0

    def bias_index(b, r):
        rs = jnp.clip(r - NA_KR // 2, 0, rows - NA_KR)
        return (rs - r + NA_KR - 1, 0, 0, 0)

    return pl.pallas_call(
        functools.partial(_natten_kernel, rows=rows, scale=NA_HEAD_DIM ** -0.5),
        grid=(B, rows),
        in_specs=[pl.BlockSpec((1, GRID_W, D), lambda b, r: (b, r, 0)),
                  pl.BlockSpec((1, N, D), lambda b, r: (b, 0, 1)),
                  pl.BlockSpec((1, N, D), lambda b, r: (b, 0, 2)),
                  pl.BlockSpec((1, L, D), lambda b, r: (b, 0, 1)),
                  pl.BlockSpec((1, L, D), lambda b, r: (b, 0, 2)),
                  pl.BlockSpec((1, NA_HEADS, GRID_W, NA_KR * GRID_W), bias_index)],
        out_specs=pl.BlockSpec((1, GRID_W, D), lambda b, r: (b, r, 0)),
        out_shape=jax.ShapeDtypeStruct((B, N, D), BF16),
        compiler_params=_params(2, VMEM_BIG),
        name="natten",
    )(qkv, qkv, qkv, qkvc, qkvc, bias_tab)


def _ctx_attn_kernel(q_ref, k_ref, v_ref, o_ref, *, scale):
    for h in range(NA_HEADS):
        sl = slice(h * NA_HEAD_DIM, (h + 1) * NA_HEAD_DIM)
        s = lax.dot_general(q_ref[0, :, sl], k_ref[0, :, sl], _NT, preferred_element_type=F32) * scale
        o_ref[0, :, sl] = _softmax_pv([s], [v_ref[0, :, sl]]).astype(o_ref.dtype)


def ctx_attention(qkvc):
    B, L, D3 = qkvc.shape
    D = D3 // 3
    return pl.pallas_call(
        functools.partial(_ctx_attn_kernel, scale=NA_HEAD_DIM ** -0.5),
        grid=(B,),
        in_specs=[pl.BlockSpec((1, L, D), lambda b: (b, 0, 0)),
                  pl.BlockSpec((1, L, D), lambda b: (b, 0, 1)),
                  pl.BlockSpec((1, L, D), lambda b: (b, 0, 2))],
        out_specs=pl.BlockSpec((1, L, D), lambda b: (b, 0, 0)),
        out_shape=jax.ShapeDtypeStruct((B, L, D), BF16),
        compiler_params=_params(1, VMEM_MID),
        name="ctx_attention",
    )(qkvc, qkvc, qkvc)


def _is_first_step():
    return (pl.program_id(0) == 0) & (pl.program_id(1) == 0)


def _mm_ln_kernel(a_ref, x_ref, gate_ref, w_ref, g_ref, b_ref, o_ref, wb_ref, *, alpha):
    @pl.when(_is_first_step())
    def _():
        wb_ref[...] = w_ref[...].astype(BF16)

    y = jnp.dot(a_ref[0], wb_ref[...], preferred_element_type=F32)
    o_ref[0] = _layernorm_rows(alpha * x_ref[0] + gate_ref[0] * y, g_ref[...], b_ref[...])


def mm_ln(a, x, gate, w, g, b, alpha, tm):
    B, N, D = x.shape
    tm = min(tm, N)
    row = lambda bi, i: (bi, i, 0)
    vec = lambda bi, i: (0, 0)
    return pl.pallas_call(
        functools.partial(_mm_ln_kernel, alpha=alpha),
        grid=(B, N // tm),
        in_specs=[pl.BlockSpec((1, tm, D), row),
                  pl.BlockSpec((1, tm, D), row),
                  pl.BlockSpec((1, 1, D), lambda bi, i: (bi, 0, 0)),
                  pl.BlockSpec((D, D), vec),
                  pl.BlockSpec((1, D), vec),
                  pl.BlockSpec((1, D), vec)],
        out_specs=pl.BlockSpec((1, tm, D), row),
        out_shape=jax.ShapeDtypeStruct((B, N, D), F32),
        scratch_shapes=[pltpu.VMEM((D, D), BF16)],
        compiler_params=_params(2, VMEM_MID),
        name="mm_ln",
    )(a, x, gate, w, g.reshape(1, D), b.reshape(1, D))


def _mm_ln_gdn_kernel(of_ref, ob_ref, z_ref, nw_ref, x_ref, gate_ref, w_ref, g_ref, b_ref, o_ref,
                      wb_ref, a_ref, *, alpha):
    @pl.when(_is_first_step())
    def _():
        wb_ref[...] = w_ref[...].astype(BF16)

    for h in range(GDN_HEADS):
        sl = slice(h * GDN_HEAD_DIM, (h + 1) * GDN_HEAD_DIM)
        o = of_ref[0, :, sl] + ob_ref[0, :, sl]
        y = o * lax.rsqrt(jnp.mean(o * o, -1, keepdims=True) + LN_EPS) * nw_ref[...]
        a_ref[:, sl] = (y * _silu(z_ref[0, :, sl])).astype(BF16)
    y = jnp.dot(a_ref[...], wb_ref[...], preferred_element_type=F32)
    o_ref[0] = _layernorm_rows(alpha * x_ref[0] + gate_ref[0] * y, g_ref[...], b_ref[...])


def mm_ln_gdn(o_f, o_b, p, norm_w, x, gate, w, g, b, alpha, tm):
    B, N, D = x.shape
    tm = min(tm, N)
    row = lambda bi, i: (bi, i, 0)
    vec = lambda bi, i: (0, 0)
    return pl.pallas_call(
        functools.partial(_mm_ln_gdn_kernel, alpha=alpha),
        grid=(B, N // tm),
        in_specs=[pl.BlockSpec((1, tm, D), row),
                  pl.BlockSpec((1, tm, D), row),
                  pl.BlockSpec((1, tm, D), lambda bi, i: (bi, i, 3)),
                  pl.BlockSpec((1, GDN_HEAD_DIM), vec),
                  pl.BlockSpec((1, tm, D), row),
                  pl.BlockSpec((1, 1, D), lambda bi, i: (bi, 0, 0)),
                  pl.BlockSpec((D, D), vec),
                  pl.BlockSpec((1, D), vec),
                  pl.BlockSpec((1, D), vec)],
        out_specs=pl.BlockSpec((1, tm, D), row),
        out_shape=jax.ShapeDtypeStruct((B, N, D), F32),
        scratch_shapes=[pltpu.VMEM((D, D), BF16), pltpu.VMEM((tm, D), BF16)],
        compiler_params=_params(2, VMEM_MID),
        name="mm_ln_gdn",
    )(o_f, o_b, p, norm_w.reshape(1, GDN_HEAD_DIM), x, gate, w, g.reshape(1, D), b.reshape(1, D))


def _res_ln_kernel(y_ref, x_ref, gate_ref, g_ref, b_ref, o_ref, *, alpha):
    o_ref[0] = _layernorm_rows(alpha * x_ref[0] + gate_ref[0] * y_ref[0], g_ref[...], b_ref[...])


def res_ln(y, x, gate, g, b, alpha, tm):
    B, N, D = x.shape
    tm = min(tm, N)
    row = lambda bi, i: (bi, i, 0)
    vec = lambda bi, i: (0, 0)
    return pl.pallas_call(
        functools.partial(_res_ln_kernel, alpha=alpha),
        grid=(B, N // tm),
        in_specs=[pl.BlockSpec((1, tm, D), row),
                  pl.BlockSpec((1, tm, D), row),
                  pl.BlockSpec((1, 1, D), lambda bi, i: (bi, 0, 0)),
                  pl.BlockSpec((1, D), vec),
                  pl.BlockSpec((1, D), vec)],
        out_specs=pl.BlockSpec((1, tm, D), row),
        out_shape=jax.ShapeDtypeStruct((B, N, D), F32),
        compiler_params=_params(2, VMEM_MID),
        name="res_ln",
    )(y, x, gate, g.reshape(1, D), b.reshape(1, D))


def _conv_kernel(prev_ref, main_ref, next_ref, w_ref, o_ref, ext_ref, *, tt, nt):
    i = pl.program_id(1)
    j = pl.program_id(2)
    ext_ref[0:CONV_HALO, :] = jnp.where(i > 0, prev_ref[0], 0.0)
    ext_ref[CONV_HALO:CONV_HALO + tt, :] = main_ref[0]
    ext_ref[CONV_HALO + tt:, :] = jnp.where(i < nt - 1, next_ref[0], 0.0)
    pad = GDN_CONV // 2
    acc = w_ref[0:1, :] * ext_ref[CONV_HALO - pad:CONV_HALO - pad + tt, :]
    for t in range(1, GDN_CONV):
        acc = acc + w_ref[t:t + 1, :] * ext_ref[CONV_HALO - pad + t:CONV_HALO - pad + t + tt, :]
    y = _silu(acc)
    qscale = jnp.where(j == 0, GDN_HEAD_DIM ** -0.5, 1.0)
    for h in range(GDN_HEADS):
        sl = slice(h * GDN_HEAD_DIM, (h + 1) * GDN_HEAD_DIM)
        seg = y[:, sl]
        inv = lax.rsqrt(jnp.sum(seg * seg, -1, keepdims=True) + 1e-6) * qscale
        o_ref[0, :, sl] = seg * jnp.where(j < 2, inv, 1.0)


def gdn_conv(p, conv_w, tt):
    B, T = p.shape[:2]
    D = conv_w.shape[1] // 3
    tt = min(tt, T)
    nt = T // tt
    hb = tt // CONV_HALO
    return pl.pallas_call(
        functools.partial(_conv_kernel, tt=tt, nt=nt),
        grid=(B, nt, 3),
        in_specs=[pl.BlockSpec((1, CONV_HALO, D), lambda b, i, j: (b, jnp.maximum(i * hb - 1, 0), j)),
                  pl.BlockSpec((1, tt, D), lambda b, i, j: (b, i, j)),
                  pl.BlockSpec((1, CONV_HALO, D), lambda b, i, j: (b, jnp.minimum((i + 1) * hb, T // CONV_HALO - 1), j)),
                  pl.BlockSpec((GDN_CONV, D), lambda b, i, j: (0, j))],
        out_specs=pl.BlockSpec((1, tt, D), lambda b, i, j: (b, i, j)),
        out_shape=jax.ShapeDtypeStruct((B, T, 3 * D), F32),
        scratch_shapes=[pltpu.VMEM((tt + 2 * CONV_HALO, D), F32)],
        compiler_params=_params(3, VMEM_MID),
        name="gdn_conv",
    )(p, p, p, conv_w)


def _gates_kernel(ab_ref, alog_ref, dtb_ref, o_ref):
    ab = ab_ref[0]
    xg = ab + dtb_ref[...]
    softplus = jnp.maximum(xg, 0.0) + jnp.log(1.0 + jnp.exp(-jnp.abs(xg)))
    g = -jnp.exp(alog_ref[...]) * softplus
    beta = jax.nn.sigmoid(ab)
    lane = lax.broadcasted_iota(I32, ab.shape, 1)
    o_ref[0] = jnp.where(lane < 2 * GDN_HEADS, g, beta)


def gdn_gates(ab, a_log, dt_bias, tt):
    B, T, W = ab.shape
    tt = min(tt, T)
    pad = lambda a: jnp.pad(a.reshape(1, -1).astype(F32), ((0, 0), (0, W - a.size)))
    return pl.pallas_call(
        _gates_kernel,
        grid=(B, T // tt),
        in_specs=[pl.BlockSpec((1, tt, W), lambda b, i: (b, i, 0)),
                  pl.BlockSpec((1, W), lambda b, i: (0, 0)),
                  pl.BlockSpec((1, W), lambda b, i: (0, 0))],
        out_specs=pl.BlockSpec((1, tt, W), lambda b, i: (b, i, 0)),
        out_shape=jax.ShapeDtypeStruct((B, T, W), F32),
        compiler_params=_params(2, VMEM_MID),
        name="gdn_gates",
    )(ab, pad(a_log), pad(dt_bias))


TRI_BASE = 16


def _unit_triangular_inverse(A, eye):
    C = A.shape[0]
    mm = lambda x, y: jnp.dot(x, y, preferred_element_type=F32)
    r = lax.broadcasted_iota(I32, (C, C), 0)
    c = lax.broadcasted_iota(I32, (C, C), 1)
    sh = int(math.log2(TRI_BASE))
    B = jnp.where((r >> sh) == (c >> sh), A, 0.0)
    T = eye - B
    P = B
    for _ in range(sh - 1):
        P = mm(P, P)
        T = T + mm(T, P)
    while sh < int(math.log2(C)):
        off = ((r >> (sh + 1)) == (c >> (sh + 1))) & ((r >> sh) != (c >> sh))
        T = T - mm(mm(T, jnp.where(off, A, 0.0)), T)
        sh += 1
    return T


def _gdn_chunk(d, h, q_ref, k_ref, v_ref, gb, Gc_all, Gr_all, S_ref, o_ref, incl, strict, eye):
    C = GDN_CHUNK
    col = d * GDN_HEADS + h
    last = C - 1 if d == 0 else 0
    Gc = Gc_all[:, col:col + 1]
    Gr = Gr_all[col:col + 1, :]
    Gtot = Gc_all[last:last + 1, col:col + 1]
    beta = gb[:, 2 * GDN_HEADS + col:2 * GDN_HEADS + col + 1]
    decay = jnp.where(incl, jnp.exp(jnp.minimum(Gc - Gr, 0.0)), 0.0)
    sl = slice(h * GDN_HEAD_DIM, (h + 1) * GDN_HEAD_DIM)
    q = q_ref[0, :, sl]
    k = k_ref[0, :, sl]
    v = v_ref[0, :, sl]
    kb = k * beta
    kq = lax.dot_general(jnp.concatenate([kb, q], axis=0), k, _NT, preferred_element_type=F32)
    A = jnp.where(strict, kq[:C] * decay, 0.0)
    attn = kq[C:] * decay
    T = _unit_triangular_inverse(A, eye)
    eG = jnp.exp(Gc)
    wu = jnp.dot(T, jnp.concatenate([kb * eG, v * beta], axis=1), preferred_element_type=F32)
    S = S_ref[d, h]
    wq = jnp.dot(jnp.concatenate([wu[:, :GDN_HEAD_DIM], q * eG], axis=0), S, preferred_element_type=F32)
    v_new = wu[:, GDN_HEAD_DIM:] - wq[:C]
    o_ref[0, :, sl] = wq[C:] + jnp.dot(attn, v_new, preferred_element_type=F32)
    k_tail = k * jnp.exp(Gtot - Gc)
    S_ref[d, h] = S * jnp.exp(Gtot) + lax.dot_general(k_tail, v_new, _TN, preferred_element_type=F32)


def _gdn_scan_kernel(qf_ref, kf_ref, vf_ref, gbf_ref, gtf_ref, qb_ref, kb_ref, vb_ref, gbb_ref, gtb_ref,
                     s0_ref, of_ref, ob_ref, sout_ref, S_ref, *, nsteps):
    i = pl.program_id(1)

    @pl.when(i == 0)
    def _():
        S_ref[...] = s0_ref[0]

    C = GDN_CHUNK
    r = lax.broadcasted_iota(I32, (C, C), 0)
    c = lax.broadcasted_iota(I32, (C, C), 1)
    eye = jnp.where(r == c, 1.0, 0.0).astype(F32)
    streams = ((qf_ref, kf_ref, vf_ref, gbf_ref, gtf_ref, of_ref, r >= c, r > c),
               (qb_ref, kb_ref, vb_ref, gbb_ref, gtb_ref, ob_ref, r <= c, r < c))
    for d, (q_ref, k_ref, v_ref, gb_ref, gt_ref, o_ref, incl, strict) in enumerate(streams):
        tri = jnp.where(incl, 1.0, 0.0).astype(F32)
        gb = gb_ref[0]
        Gc_all = jnp.dot(tri, gb, preferred_element_type=F32, precision=HIGHEST)
        Gr_all = lax.dot_general(gt_ref[0], tri, _NT, preferred_element_type=F32, precision=HIGHEST)
        for h in range(GDN_HEADS):
            _gdn_chunk(d, h, q_ref, k_ref, v_ref, gb, Gc_all, Gr_all, S_ref, o_ref, incl, strict, eye)

    @pl.when(i == nsteps - 1)
    def _():
        sout_ref[0] = S_ref[...]


def gdn_scan(qkv, gb, gbt, s0):
    B, T, D3 = qkv.shape
    D = D3 // 3
    C = GDN_CHUNK
    n = T // C
    W = gb.shape[2]
    R = gbt.shape[1]
    fwd = lambda j: (lambda b, i: (b, i, j))
    bwd = lambda j: (lambda b, i: (b, n - 1 - i, j))
    st = lambda b, i: (b, 0, 0, 0, 0)
    sspec = pl.BlockSpec((1, 2, GDN_HEADS, GDN_HEAD_DIM, GDN_HEAD_DIM), st)
    return pl.pallas_call(
        functools.partial(_gdn_scan_kernel, nsteps=n),
        grid=(B, n),
        in_specs=[pl.BlockSpec((1, C, D), fwd(0)), pl.BlockSpec((1, C, D), fwd(1)), pl.BlockSpec((1, C, D), fwd(2)),
                  pl.BlockSpec((1, C, W), fwd(0)), pl.BlockSpec((1, R, C), lambda b, i: (b, 0, i)),
                  pl.BlockSpec((1, C, D), bwd(0)), pl.BlockSpec((1, C, D), bwd(1)), pl.BlockSpec((1, C, D), bwd(2)),
                  pl.BlockSpec((1, C, W), bwd(0)), pl.BlockSpec((1, R, C), lambda b, i: (b, 0, n - 1 - i)),
                  sspec],
        out_specs=[pl.BlockSpec((1, C, D), fwd(0)), pl.BlockSpec((1, C, D), bwd(0)), sspec],
        out_shape=[jax.ShapeDtypeStruct((B, T, D), F32), jax.ShapeDtypeStruct((B, T, D), F32),
                   jax.ShapeDtypeStruct(s0.shape, F32)],
        scratch_shapes=[pltpu.VMEM((2, GDN_HEADS, GDN_HEAD_DIM, GDN_HEAD_DIM), F32)],
        compiler_params=_params(2, VMEM_MID),
        name="gdn_scan",
    )(qkv, qkv, qkv, gb, gbt, qkv, qkv, qkv, gb, gbt, s0)


def _router_kernel(x_ref, sc_ref, sh_ref, wr_ref, o_ref):
    h = x_ref[0] * (1.0 + sc_ref[0]) + sh_ref[0]
    lg = lax.dot_general(wr_ref[...], h, _NT, preferred_element_type=F32, precision=HIGHEST)
    e = jnp.exp(lg - jnp.max(lg, 0, keepdims=True))
    o_ref[0] = e / jnp.sum(e, 0, keepdims=True)


def router(x, sc, sh, w_router_t, tt):
    B, n, D = x.shape
    E = w_router_t.shape[0]
    tt = min(tt, n)
    return pl.pallas_call(
        _router_kernel,
        grid=(B, n // tt),
        in_specs=[pl.BlockSpec((1, tt, D), lambda b, i: (b, i, 0)),
                  pl.BlockSpec((1, 1, D), lambda b, i: (b, 0, 0)),
                  pl.BlockSpec((1, 1, D), lambda b, i: (b, 0, 0)),
                  pl.BlockSpec((E, D), lambda b, i: (0, 0))],
        out_specs=pl.BlockSpec((1, E, tt), lambda b, i: (b, 0, i)),
        out_shape=jax.ShapeDtypeStruct((B, E, n), F32),
        compiler_params=_params(2, VMEM_MID),
        name="router",
    )(x, sc, sh, w_router_t)


def _exclusive_cumsum_lanes(mask, n):
    r = lax.broadcasted_iota(I32, (LANES, LANES), 0)
    c = lax.broadcasted_iota(I32, (LANES, LANES), 1)
    upper = jnp.where(r < c, 1.0, 0.0).astype(BF16)
    carry = jnp.zeros((mask.shape[0], 1), F32)
    out = []
    for j in range(n // LANES):
        blk = mask[:, j * LANES:(j + 1) * LANES]
        out.append(jnp.dot(blk.astype(BF16), upper, preferred_element_type=F32) + carry)
        carry = carry + jnp.sum(blk, -1, keepdims=True)
    return jnp.concatenate(out, axis=1)


def _topk_kernel(aff_ref, idx_ref, gate_ref, pos_ref, lhs_ref, res_ref, *, n, cap, tb):
    E = aff_ref.shape[1]
    aff = aff_ref[0]
    bits = pltpu.bitcast(aff, I32)
    count = lambda m: jnp.sum(jnp.where(m, 1.0, 0.0), -1, keepdims=True)
    t = jnp.zeros((E, 1), I32)
    for bit in range(30, -1, -1):
        cand = t | (1 << bit)
        t = jnp.where(count(bits >= cand) >= cap, cand, t)
    gt = bits > t
    eq = bits == t
    need = cap - count(gt)
    eq_rank = _exclusive_cumsum_lanes(jnp.where(eq, 1.0, 0.0), n)
    sel = gt | (eq & (eq_rank < need))
    pos = _exclusive_cumsum_lanes(jnp.where(sel, 1.0, 0.0), n)
    pos_ref[...] = jnp.where(sel, pos, -1.0)
    tok = lax.broadcasted_iota(I32, (E, n), 1)
    row = lax.broadcasted_iota(I32, (E, n), 0)
    tokrows = jnp.where(row == 0, tok // 64, jnp.where(row == 1, tok % 64, 0)).astype(F32)
    g_hi = aff.astype(BF16)
    r1 = aff - g_hi.astype(F32)
    g_mid = r1.astype(BF16)
    g_lo = (r1 - g_mid.astype(F32)).astype(BF16)
    lhs_ref[0 * E:1 * E, :] = tokrows.astype(BF16)
    lhs_ref[1 * E:2 * E, :] = g_hi
    lhs_ref[2 * E:3 * E, :] = g_mid
    lhs_ref[3 * E:4 * E, :] = g_lo
    slot = lax.broadcasted_iota(I32, (cap, tb), 0).astype(F32)

    def per_expert(e, carry):
        acc = jnp.zeros((4 * E, cap), F32)
        for j in range(n // tb):
            blk = slice(j * tb, (j + 1) * tb)
            onehot = jnp.where(pos_ref[pl.ds(e, 1), blk] == slot, 1.0, 0.0).astype(BF16)
            acc = acc + lax.dot_general(lhs_ref[:, blk], onehot, _NT, preferred_element_type=F32)
        res_ref[...] = acc
        idx_row = res_ref[0:1, :] * 64.0 + res_ref[1:2, :]
        idx_ref[0, pl.ds(e, 1), :] = idx_row.astype(I32)
        gate_ref[0, pl.ds(e, 1), :] = (res_ref[pl.ds(E + e, 1), :] + res_ref[pl.ds(2 * E + e, 1), :]
                                       + res_ref[pl.ds(3 * E + e, 1), :])
        return carry

    lax.fori_loop(0, E, per_expert, 0)


def topk_select(aff, cap):
    B, E, n = aff.shape
    tb = min(512, n)
    return pl.pallas_call(
        functools.partial(_topk_kernel, n=n, cap=cap, tb=tb),
        grid=(B,),
        in_specs=[pl.BlockSpec((1, E, n), lambda b: (b, 0, 0))],
        out_specs=[pl.BlockSpec((1, E, cap), lambda b: (b, 0, 0)),
                   pl.BlockSpec((1, E, cap), lambda b: (b, 0, 0))],
        out_shape=[jax.ShapeDtypeStruct((B, E, cap), I32), jax.ShapeDtypeStruct((B, E, cap), F32)],
        scratch_shapes=[pltpu.VMEM((E, n), F32), pltpu.VMEM((4 * E, n), BF16), pltpu.VMEM((4 * E, cap), F32)],
        compiler_params=_params(1, VMEM_MID),
        name="topk_select",
    )(aff)


def _moe_kernel(idx_ref, gate_ref, x_ref, sc_ref, sh_ref, wg_ref, wu_ref, wd_ref, y_ref,
                xs32_ref, xs_ref, ye_ref, *, S, nfc):
    e = pl.program_id(0)
    f = pl.program_id(1)
    D = x_ref.shape[-1]
    sub = lax.broadcasted_iota(I32, (SUBLANES, D), 0)

    @pl.when((e == 0) & (f == 0))
    def _():
        y_ref[0] = jnp.zeros(y_ref.shape[1:], F32)

    @pl.when(f == 0)
    def _():
        def gather8(s8, carry):
            base = pl.multiple_of(s8 * SUBLANES, SUBLANES)
            tile = jnp.zeros((SUBLANES, D), F32)
            for j in range(SUBLANES):
                i = idx_ref[e * S + base + j]
                chunk = x_ref[0, pl.ds(pl.multiple_of((i >> 3) << 3, SUBLANES), SUBLANES), :]
                tile = jnp.where(sub == j, pltpu.roll(chunk, (j - (i & 7)) & 7, 0), tile)
            xs32_ref[pl.ds(base, SUBLANES), :] = tile
            return carry

        lax.fori_loop(0, S // SUBLANES, gather8, 0)
        xs_ref[...] = (xs32_ref[...] * (1.0 + sc_ref[0]) + sh_ref[0]).astype(BF16)

    xs = xs_ref[...]
    a = jnp.dot(xs, wg_ref[0], preferred_element_type=F32)
    u = jnp.dot(xs, wu_ref[0], preferred_element_type=F32)
    part = jnp.dot((_silu(a) * u).astype(BF16), wd_ref[0], preferred_element_type=F32)

    @pl.when(f == 0)
    def _():
        ye_ref[...] = part

    @pl.when(f > 0)
    def _():
        ye_ref[...] += part

    @pl.when(f == nfc - 1)
    def _():
        def scatter8(s8, carry):
            base = pl.multiple_of(s8 * SUBLANES, SUBLANES)
            ye8 = ye_ref[pl.ds(base, SUBLANES), :]
            for j in range(SUBLANES):
                i = idx_ref[e * S + base + j]
                g = gate_ref[e * S + base + j]
                rows = pl.ds(pl.multiple_of((i >> 3) << 3, SUBLANES), SUBLANES)
                contrib = jnp.where(sub == (i & 7), pltpu.roll(ye8, ((i & 7) - j) & 7, 0) * g, 0.0)
                y_ref[0, rows, :] = y_ref[0, rows, :] + contrib
            return carry

        lax.fori_loop(0, S // SUBLANES, scatter8, 0)


def moe_experts(idx, gate, x, g, sc, sh, wg, wu, wd, fc):
    G, R, D = x.shape
    E, _, F = wg.shape
    S = idx.shape[0] // E
    nfc = F // fc
    return pl.pallas_call(
        functools.partial(_moe_kernel, S=S, nfc=nfc),
        grid_spec=pltpu.PrefetchScalarGridSpec(
            num_scalar_prefetch=2,
            grid=(E, nfc),
            in_specs=[pl.BlockSpec((1, R, D), lambda e, f, *_: (g, 0, 0), pipeline_mode=pl.Buffered(1)),
                      pl.BlockSpec((1, 1, D), lambda e, f, *_: (g, 0, 0)),
                      pl.BlockSpec((1, 1, D), lambda e, f, *_: (g, 0, 0)),
                      pl.BlockSpec((1, D, fc), lambda e, f, *_: (e, 0, f)),
                      pl.BlockSpec((1, D, fc), lambda e, f, *_: (e, 0, f)),
                      pl.BlockSpec((1, fc, D), lambda e, f, *_: (e, f, 0))],
            out_specs=pl.BlockSpec((1, R, D), lambda e, f, *_: (0, 0, 0), pipeline_mode=pl.Buffered(1)),
            scratch_shapes=[pltpu.VMEM((S, D), F32), pltpu.VMEM((S, D), BF16), pltpu.VMEM((S, D), F32)]),
        out_shape=jax.ShapeDtypeStruct((1, R, D), F32),
        compiler_params=_params(2, VMEM_BIG),
        name="moe_experts",
    )(idx, gate, x, sc, sh, wg, wu, wd)[0]


def ec_moe(x, sc, sh, w_router_t, wg, wu, wd, flatten_groups):
    B, n, D = x.shape
    E = w_router_t.shape[0]
    cap = EC_CAPACITY_FACTOR * n // E
    aff = router(x, sc, sh, w_router_t, 512)
    idx, gate = topk_select(aff, cap)
    fc = 512
    if flatten_groups:
        idx = idx + (jnp.arange(B, dtype=I32) * n)[:, None, None]
        idx = idx.transpose(1, 0, 2).reshape(-1)
        gate = gate.transpose(1, 0, 2).reshape(-1)
        y = moe_experts(idx, gate, x.reshape(1, B * n, D), 0, sc[:1], sh[:1], wg, wu, wd, fc)
        return y.reshape(B, n, D)
    ys = [moe_experts(idx[b].reshape(-1), gate[b].reshape(-1), x, b, sc, sh, wg, wu, wd, fc) for b in range(B)]
    return jnp.stack(ys)


def kernel(x, c, ctx, c_ctx, ada_w, ada_b, ln_g, ln_b, na_w_qkv, na_w_o, na_rpb, gdn_w_in, gdn_conv_w,
           gdn_a_log, gdn_dt_bias, gdn_norm_w, gdn_w_o, moe_w_router, moe_w_gate, moe_w_up, moe_w_down):
    B, N, D = x.shape
    L = ctx.shape[1]
    depth = ada_w.shape[0]
    alpha = (2.0 * depth) ** 0.25
    xc = ctx

    rows = -(-(B + 1) // SUBLANES) * SUBLANES
    cs = jnp.zeros((rows, D), F32).at[:B].set(c).at[B].set(c_ctx)
    mods = ada_modulation(cs, ada_w, ada_b)

    wg_all = moe_w_gate.astype(BF16)
    wu_all = moe_w_up.astype(BF16)
    wd_all = moe_w_down.astype(BF16)

    for l in range(depth):
        last = l == depth - 1
        i = l // 2
        mod = [mods[l, :B, j * D:(j + 1) * D].reshape(B, 1, D) for j in range(6)]
        modc = [jnp.broadcast_to(mods[l, B, j * D:(j + 1) * D].reshape(1, 1, D), (B, 1, D)) for j in range(6)]
        g1, b1, g2, b2 = ln_g[l, 0], ln_b[l, 0], ln_g[l, 1], ln_b[l, 1]
        if l % 2 == 0:
            qkv = mm_mod(x, mod[1], mod[0], na_w_qkv[i], BF16, 1024, 1024)
            qkvc = mm_mod(xc, modc[1], modc[0], na_w_qkv[i], BF16, 1024, 1024)
            o = natten(qkv, qkvc, _natten_bias_table(na_rpb[i]))
            x = mm_ln(o, x, mod[2], na_w_o[i], g1, b1, alpha, 512)
            if not last:
                xc = mm_ln(ctx_attention(qkvc), xc, modc[2], na_w_o[i], g1, b1, alpha, 512)
        else:
            w_main = gdn_w_in[i][:, :4 * D]
            w_ab = jnp.pad(gdn_w_in[i][:, 4 * D:], ((0, 0), (0, LANES - 4 * GDN_HEADS)))

            def project(u, m):
                p = mm_mod(u, m[1], m[0], w_main, F32, 1024, 1024)
                ab = mm_mod(u, m[1], m[0], w_ab, F32, 1024, LANES)
                gb = gdn_gates(ab, gdn_a_log[i], gdn_dt_bias[i], 1024)
                gbt = jnp.swapaxes(gb[:, :, :4 * GDN_HEADS], 1, 2)
                return p, gdn_conv(p, gdn_conv_w[i], 512), gb, gbt

            p, qkv, gb, gbt = project(x, mod)
            pc, qkvc, gbc, gbtc = project(xc, modc)
            s0 = jnp.zeros((B, 2, GDN_HEADS, GDN_HEAD_DIM, GDN_HEAD_DIM), F32)
            oc_f, oc_b, s_ctx = gdn_scan(qkvc, gbc, gbtc, s0)
            o_f, o_b, _ = gdn_scan(qkv, gb, gbt, s_ctx)
            x = mm_ln_gdn(o_f, o_b, p, gdn_norm_w[i], x, mod[2], gdn_w_o[i], g1, b1, alpha, 512)
            if not last:
                xc = mm_ln_gdn(oc_f, oc_b, pc, gdn_norm_w[i], xc, modc[2], gdn_w_o[i], g1, b1, alpha, 512)
        wr_t = moe_w_router[l].T
        y = ec_moe(x, mod[4], mod[3], wr_t, wg_all[l], wu_all[l], wd_all[l], False)
        x = res_ln(y, x, mod[5], g2, b2, alpha, 512)
        if not last:
            yc = ec_moe(xc, modc[4], modc[3], wr_t, wg_all[l], wu_all[l], wd_all[l], True)
            xc = res_ln(yc, xc, modc[5], g2, b2, alpha, 512)
    return x
```

```python
import functools
import math

import numpy as np
import jax
import jax.numpy as jnp
from jax import lax
from jax.experimental import pallas as pl
from jax.experimental.pallas import tpu as pltpu

F32 = jnp.float32
BF16 = jnp.bfloat16
I32 = jnp.int32
HIGHEST = lax.Precision.HIGHEST

GRID_W = 64
NA_HEADS = 16
NA_HEAD_DIM = 64
NA_KR = 8
NA_KC = 16
GDN_HEADS = 8
GDN_HEAD_DIM = 128
GDN_CONV = 5
N_EXPERTS = 16
EC_CAPACITY_FACTOR = 2
LN_EPS = 1e-6
NEG_INF = -1e30

GDN_CHUNK = 128
CONV_HALO = 8
SUBLANES = 8
LANES = 128
V7X_VMEM_BYTES = 64 * 1024 * 1024
VMEM_BIG = 56 * 1024 * 1024
VMEM_MID = 40 * 1024 * 1024

_NT = (((1,), (1,)), ((), ()))
_TN = (((0,), (0,)), ((), ()))


def _params(n_axes, vmem):
    return pltpu.CompilerParams(dimension_semantics=("arbitrary",) * n_axes, vmem_limit_bytes=vmem)


def _silu(x):
    return x * jax.nn.sigmoid(x)


def _layernorm_rows(z, g, b):
    mu = jnp.mean(z, -1, keepdims=True)
    zc = z - mu
    var = jnp.mean(zc * zc, -1, keepdims=True)
    return zc * lax.rsqrt(var + LN_EPS) * g + b


def _ada_kernel(cs_ref, w_ref, b_ref, o_ref):
    s = _silu(cs_ref[...])
    o_ref[0] = jnp.dot(s, w_ref[0], preferred_element_type=F32, precision=HIGHEST) + b_ref[0]


def ada_modulation(cs, ada_w, ada_b):
    depth, D, D6 = ada_w.shape
    R = cs.shape[0]
    tn = D6 // 4
    return pl.pallas_call(
        _ada_kernel,
        grid=(depth, D6 // tn),
        in_specs=[pl.BlockSpec((R, D), lambda l, j: (0, 0)),
                  pl.BlockSpec((1, D, tn), lambda l, j: (l, 0, j)),
                  pl.BlockSpec((1, 1, tn), lambda l, j: (l, 0, j))],
        out_specs=pl.BlockSpec((1, R, tn), lambda l, j: (l, 0, j)),
        out_shape=jax.ShapeDtypeStruct((depth, R, D6), F32),
        compiler_params=_params(2, VMEM_MID),
        name="ada_modulation",
    )(cs, ada_w, ada_b.reshape(depth, 1, D6))


def _mm_mod_kernel(x_ref, sc_ref, sh_ref, w_ref, o_ref, wb_ref):
    @pl.when((pl.program_id(1) == 0) & (pl.program_id(2) == 0))
    def _():
        wb_ref[...] = w_ref[...].astype(BF16)

    h = x_ref[0] * (1.0 + sc_ref[0]) + sh_ref[0]
    o_ref[0] = jnp.dot(h.astype(BF16), wb_ref[...], preferred_element_type=F32).astype(o_ref.dtype)


def mm_mod(x, sc, sh, w, out_dtype, tm, tn):
    B, N, D = x.shape
    NO = w.shape[1]
    tm = min(tm, N)
    tn = min(tn, NO)
    return pl.pallas_call(
        _mm_mod_kernel,
        grid=(NO // tn, B, N // tm),
        in_specs=[pl.BlockSpec((1, tm, D), lambda j, b, i: (b, i, 0)),
                  pl.BlockSpec((1, 1, D), lambda j, b, i: (b, 0, 0)),
                  pl.BlockSpec((1, 1, D), lambda j, b, i: (b, 0, 0)),
                  pl.BlockSpec((D, tn), lambda j, b, i: (0, j))],
        out_specs=pl.BlockSpec((1, tm, tn), lambda j, b, i: (b, i, j)),
        out_shape=jax.ShapeDtypeStruct((B, N, NO), out_dtype),
        scratch_shapes=[pltpu.VMEM((D, tn), BF16)],
        compiler_params=_params(3, VMEM_MID),
        name="mm_mod",
    )(x, sc, sh, w)


def _natten_bias_table(rpb):
    H = rpb.shape[0]
    qc = np.arange(GRID_W)
    kc = np.arange(GRID_W)
    col_start = np.clip(qc - NA_KC // 2, 0, GRID_W - NA_KC)
    valid = (kc[None, :] >= col_start[:, None]) & (kc[None, :] < col_start[:, None] + NA_KC)
    dc = np.clip(kc[None, :] - qc[:, None], -(NA_KC - 1), NA_KC - 1) + NA_KC - 1
    tab = rpb.astype(F32)[:, :, dc] + jnp.where(jnp.asarray(valid), 0.0, NEG_INF).astype(F32)[None, None]
    variants = [tab[:, d0:d0 + NA_KR].transpose(0, 2, 1, 3).reshape(H, GRID_W, NA_KR * GRID_W)
                for d0 in range(NA_KR)]
    return jnp.stack(variants)


def _softmax_pv(s_list, v_list):
    m = functools.reduce(jnp.maximum, [jnp.max(s, -1, keepdims=True) for s in s_list])
    ps = [jnp.exp(s - m) for s in s_list]
    den = functools.reduce(lambda a, b: a + b, [jnp.sum(p, -1, keepdims=True) for p in ps])
    o = functools.reduce(lambda a, b: a + b,
                         [jnp.dot(p.astype(BF16), v, preferred_element_type=F32) for p, v in zip(ps, v_list)])
    return o / den


def _natten_kernel(q_ref, k_ref, v_ref, kc_ref, vc_ref, bias_ref, o_ref, *, rows, scale):
    r = pl.program_id(1)
    rs = jnp.clip(r - NA_KR // 2, 0, rows - NA_KR)
    start = pl.multiple_of(rs * GRID_W, GRID_W)
    win = pl.ds(start, NA_KR * GRID_W)
    lane = lax.broadcasted_iota(I32, (GRID_W, LANES), 1)
    per_tile = LANES // NA_HEAD_DIM
    masks = [(lane >= half * NA_HEAD_DIM) & (lane < (half + 1) * NA_HEAD_DIM) for half in range(per_tile)]
    tiles = [slice(pair * LANES, (pair + 1) * LANES) for pair in range(NA_HEADS // per_tile)]
    scores = []
    for pair, sl in enumerate(tiles):
        q2 = q_ref[0, :, sl] * scale
        for half in range(per_tile):
            qh = jnp.where(masks[half], q2, jnp.zeros_like(q2))
            s_w = lax.dot_general(qh, k_ref[0, win, sl], _NT, preferred_element_type=F32)
            s_c = lax.dot_general(qh, kc_ref[0, :, sl], _NT, preferred_element_type=F32)
            scores.append((s_w + bias_ref[0, pair * per_tile + half], s_c))
    maxes = [jnp.maximum(jnp.max(s_w, -1, keepdims=True), jnp.max(s_c, -1, keepdims=True)) for s_w, s_c in scores]
    probs = [(jnp.exp(s_w - m), jnp.exp(s_c - m)) for (s_w, s_c), m in zip(scores, maxes)]
    dens = [jnp.sum(p_w, -1, keepdims=True) + jnp.sum(p_c, -1, keepdims=True) for p_w, p_c in probs]
    for pair, sl in enumerate(tiles):
        o = None
        for half in range(per_tile):
            h = pair * per_tile + half
            p_w, p_c = probs[h]
            pv = (jnp.dot(p_w.astype(BF16), v_ref[0, win, sl], preferred_element_type=F32)
                  + jnp.dot(p_c.astype(BF16), vc_ref[0, :, sl], preferred_element_type=F32)) / dens[h]
            o = pv if o is None else jnp.where(masks[half], pv, o)
        o_ref[0, :, sl] = o.astype(o_ref.dtype)


def natten(qkv, qkvc, bias_tab):
    B, N, D3 = qkv.shape
    D = D3 // 3
    L = qkvc.shape[1]
    rows = N // GRID_W
    assert rows >= NA_KR and N % GRID_W == 0
    scale = NA_HEAD_DIM ** -0.5
    assert math.frexp(scale)[0] == 0.5, "q is pre-scaled in bf16: the scale must be a power of two"

    def bias_index(b, r):
        rs = jnp.clip(r - NA_KR // 2, 0, rows - NA_KR)
        return (rs - r + NA_KR - 1, 0, 0, 0)

    return pl.pallas_call(
        functools.partial(_natten_kernel, rows=rows, scale=scale),
        grid=(B, rows),
        in_specs=[pl.BlockSpec((1, GRID_W, D), lambda b, r: (b, r, 0)),
                  pl.BlockSpec((1, N, D), lambda b, r: (b, 0, 1)),
                  pl.BlockSpec((1, N, D), lambda b, r: (b, 0, 2)),
                  pl.BlockSpec((1, L, D), lambda b, r: (b, 0, 1)),
                  pl.BlockSpec((1, L, D), lambda b, r: (b, 0, 2)),
                  pl.BlockSpec((1, NA_HEADS, GRID_W, NA_KR * GRID_W), bias_index)],
        out_specs=pl.BlockSpec((1, GRID_W, D), lambda b, r: (b, r, 0)),
        out_shape=jax.ShapeDtypeStruct((B, N, D), BF16),
        compiler_params=_params(2, VMEM_BIG),
        name="natten",
    )(qkv, qkv, qkv, qkvc, qkvc, bias_tab)


def _ctx_attn_kernel(q_ref, k_ref, v_ref, o_ref, *, scale):
    for h in range(NA_HEADS):
        sl = slice(h * NA_HEAD_DIM, (h + 1) * NA_HEAD_DIM)
        s = lax.dot_general(q_ref[0, :, sl], k_ref[0, :, sl], _NT, preferred_element_type=F32) * scale
        o_ref[0, :, sl] = _softmax_pv([s], [v_ref[0, :, sl]]).astype(o_ref.dtype)


def ctx_attention(qkvc):
    B, L, D3 = qkvc.shape
    D = D3 // 3
    return pl.pallas_call(
        functools.partial(_ctx_attn_kernel, scale=NA_HEAD_DIM ** -0.5),
        grid=(B,),
        in_specs=[pl.BlockSpec((1, L, D), lambda b: (b, 0, 0)),
                  pl.BlockSpec((1, L, D), lambda b: (b, 0, 1)),
                  pl.BlockSpec((1, L, D), lambda b: (b, 0, 2))],
        out_specs=pl.BlockSpec((1, L, D), lambda b: (b, 0, 0)),
        out_shape=jax.ShapeDtypeStruct((B, L, D), BF16),
        compiler_params=_params(1, VMEM_MID),
        name="ctx_attention",
    )(qkvc, qkvc, qkvc)


def _is_first_step():
    return (pl.program_id(0) == 0) & (pl.program_id(1) == 0)


def _mm_ln_kernel(a_ref, x_ref, gate_ref, w_ref, g_ref, b_ref, o_ref, wb_ref, *, alpha):
    @pl.when(_is_first_step())
    def _():
        wb_ref[...] = w_ref[...].astype(BF16)

    y = jnp.dot(a_ref[0], wb_ref[...], preferred_element_type=F32)
    o_ref[0] = _layernorm_rows(alpha * x_ref[0] + gate_ref[0] * y, g_ref[...], b_ref[...])


def mm_ln(a, x, gate, w, g, b, alpha, tm):
    B, N, D = x.shape
    tm = min(tm, N)
    row = lambda bi, i: (bi, i, 0)
    vec = lambda bi, i: (0, 0)
    return pl.pallas_call(
        functools.partial(_mm_ln_kernel, alpha=alpha),
        grid=(B, N // tm),
        in_specs=[pl.BlockSpec((1, tm, D), row),
                  pl.BlockSpec((1, tm, D), row),
                  pl.BlockSpec((1, 1, D), lambda bi, i: (bi, 0, 0)),
                  pl.BlockSpec((D, D), vec),
                  pl.BlockSpec((1, D), vec),
                  pl.BlockSpec((1, D), vec)],
        out_specs=pl.BlockSpec((1, tm, D), row),
        out_shape=jax.ShapeDtypeStruct((B, N, D), F32),
        scratch_shapes=[pltpu.VMEM((D, D), BF16)],
        compiler_params=_params(2, VMEM_MID),
        name="mm_ln",
    )(a, x, gate, w, g.reshape(1, D), b.reshape(1, D))


def _mm_ln_gdn_kernel(of_ref, ob_ref, z_ref, nw_ref, x_ref, gate_ref, w_ref, g_ref, b_ref, o_ref,
                      wb_ref, a_ref, *, alpha):
    @pl.when(_is_first_step())
    def _():
        wb_ref[...] = w_ref[...].astype(BF16)

    for h in range(GDN_HEADS):
        sl = slice(h * GDN_HEAD_DIM, (h + 1) * GDN_HEAD_DIM)
        o = of_ref[0, :, sl] + ob_ref[0, :, sl]
        y = o * lax.rsqrt(jnp.mean(o * o, -1, keepdims=True) + LN_EPS) * nw_ref[...]
        a_ref[:, sl] = (y * _silu(z_ref[0, :, sl])).astype(BF16)
    y = jnp.dot(a_ref[...], wb_ref[...], preferred_element_type=F32)
    o_ref[0] = _layernorm_rows(alpha * x_ref[0] + gate_ref[0] * y, g_ref[...], b_ref[...])


def mm_ln_gdn(o_f, o_b, p, norm_w, x, gate, w, g, b, alpha, tm):
    B, N, D = x.shape
    tm = min(tm, N)
    row = lambda bi, i: (bi, i, 0)
    vec = lambda bi, i: (0, 0)
    return pl.pallas_call(
        functools.partial(_mm_ln_gdn_kernel, alpha=alpha),
        grid=(B, N // tm),
        in_specs=[pl.BlockSpec((1, tm, D), row),
                  pl.BlockSpec((1, tm, D), row),
                  pl.BlockSpec((1, tm, D), lambda bi, i: (bi, i, 3)),
                  pl.BlockSpec((1, GDN_HEAD_DIM), vec),
                  pl.BlockSpec((1, tm, D), row),
                  pl.BlockSpec((1, 1, D), lambda bi, i: (bi, 0, 0)),
                  pl.BlockSpec((D, D), vec),
                  pl.BlockSpec((1, D), vec),
                  pl.BlockSpec((1, D), vec)],
        out_specs=pl.BlockSpec((1, tm, D), row),
        out_shape=jax.ShapeDtypeStruct((B, N, D), F32),
        scratch_shapes=[pltpu.VMEM((D, D), BF16), pltpu.VMEM((tm, D), BF16)],
        compiler_params=_params(2, VMEM_MID),
        name="mm_ln_gdn",
    )(o_f, o_b, p, norm_w.reshape(1, GDN_HEAD_DIM), x, gate, w, g.reshape(1, D), b.reshape(1, D))


def _res_ln_kernel(y_ref, x_ref, gate_ref, g_ref, b_ref, o_ref, *, alpha):
    o_ref[0] = _layernorm_rows(alpha * x_ref[0] + gate_ref[0] * y_ref[0], g_ref[...], b_ref[...])


def res_ln(y, x, gate, g, b, alpha, tm):
    B, N, D = x.shape
    tm = min(tm, N)
    row = lambda bi, i: (bi, i, 0)
    vec = lambda bi, i: (0, 0)
    return pl.pallas_call(
        functools.partial(_res_ln_kernel, alpha=alpha),
        grid=(B, N // tm),
        in_specs=[pl.BlockSpec((1, tm, D), row),
                  pl.BlockSpec((1, tm, D), row),
                  pl.BlockSpec((1, 1, D), lambda bi, i: (bi, 0, 0)),
                  pl.BlockSpec((1, D), vec),
                  pl.BlockSpec((1, D), vec)],
        out_specs=pl.BlockSpec((1, tm, D), row),
        out_shape=jax.ShapeDtypeStruct((B, N, D), F32),
        compiler_params=_params(2, VMEM_MID),
        name="res_ln",
    )(y, x, gate, g.reshape(1, D), b.reshape(1, D))


def _conv_kernel(prev_ref, main_ref, next_ref, w_ref, o_ref, ext_ref, *, tt, nt):
    i = pl.program_id(1)
    j = pl.program_id(2)
    ext_ref[0:CONV_HALO, :] = jnp.where(i > 0, prev_ref[0], 0.0)
    ext_ref[CONV_HALO:CONV_HALO + tt, :] = main_ref[0]
    ext_ref[CONV_HALO + tt:, :] = jnp.where(i < nt - 1, next_ref[0], 0.0)
    pad = GDN_CONV // 2
    acc = w_ref[0:1, :] * ext_ref[CONV_HALO - pad:CONV_HALO - pad + tt, :]
    for t in range(1, GDN_CONV):
        acc = acc + w_ref[t:t + 1, :] * ext_ref[CONV_HALO - pad + t:CONV_HALO - pad + t + tt, :]
    y = _silu(acc)
    qscale = jnp.where(j == 0, GDN_HEAD_DIM ** -0.5, 1.0)
    for h in range(GDN_HEADS):
        sl = slice(h * GDN_HEAD_DIM, (h + 1) * GDN_HEAD_DIM)
        seg = y[:, sl]
        inv = lax.rsqrt(jnp.sum(seg * seg, -1, keepdims=True) + 1e-6) * qscale
        o_ref[0, :, sl] = seg * jnp.where(j < 2, inv, 1.0)


def gdn_conv(p, conv_w, tt):
    B, T = p.shape[:2]
    D = conv_w.shape[1] // 3
    tt = min(tt, T)
    nt = T // tt
    hb = tt // CONV_HALO
    return pl.pallas_call(
        functools.partial(_conv_kernel, tt=tt, nt=nt),
        grid=(B, nt, 3),
        in_specs=[pl.BlockSpec((1, CONV_HALO, D), lambda b, i, j: (b, jnp.maximum(i * hb - 1, 0), j)),
                  pl.BlockSpec((1, tt, D), lambda b, i, j: (b, i, j)),
                  pl.BlockSpec((1, CONV_HALO, D), lambda b, i, j: (b, jnp.minimum((i + 1) * hb, T // CONV_HALO - 1), j)),
                  pl.BlockSpec((GDN_CONV, D), lambda b, i, j: (0, j))],
        out_specs=pl.BlockSpec((1, tt, D), lambda b, i, j: (b, i, j)),
        out_shape=jax.ShapeDtypeStruct((B, T, 3 * D), F32),
        scratch_shapes=[pltpu.VMEM((tt + 2 * CONV_HALO, D), F32)],
        compiler_params=_params(3, VMEM_MID),
        name="gdn_conv",
    )(p, p, p, conv_w)


def _gates_kernel(ab_ref, alog_ref, dtb_ref, o_ref):
    ab = ab_ref[0]
    xg = ab + dtb_ref[...]
    softplus = jnp.maximum(xg, 0.0) + jnp.log(1.0 + jnp.exp(-jnp.abs(xg)))
    g = -jnp.exp(alog_ref[...]) * softplus
    beta = jax.nn.sigmoid(ab)
    lane = lax.broadcasted_iota(I32, ab.shape, 1)
    o_ref[0] = jnp.where(lane < 2 * GDN_HEADS, g, beta)


def gdn_gates(ab, a_log, dt_bias, tt):
    B, T, W = ab.shape
    tt = min(tt, T)
    pad = lambda a: jnp.pad(a.reshape(1, -1).astype(F32), ((0, 0), (0, W - a.size)))
    return pl.pallas_call(
        _gates_kernel,
        grid=(B, T // tt),
        in_specs=[pl.BlockSpec((1, tt, W), lambda b, i: (b, i, 0)),
                  pl.BlockSpec((1, W), lambda b, i: (0, 0)),
                  pl.BlockSpec((1, W), lambda b, i: (0, 0))],
        out_specs=pl.BlockSpec((1, tt, W), lambda b, i: (b, i, 0)),
        out_shape=jax.ShapeDtypeStruct((B, T, W), F32),
        compiler_params=_params(2, VMEM_MID),
        name="gdn_gates",
    )(ab, pad(a_log), pad(dt_bias))


TRI_BASE = 16


def _bmm(a, b):
    return jnp.einsum("hij,hjk->hik", a.astype(BF16), b.astype(BF16), preferred_element_type=F32)


def _unit_triangular_inverse(A, eye):
    C = A.shape[-1]
    r = lax.broadcasted_iota(I32, (C, C), 0)
    c = lax.broadcasted_iota(I32, (C, C), 1)
    sh = int(math.log2(TRI_BASE))
    B = jnp.where((r >> sh) == (c >> sh), A, 0.0)
    T = eye - B
    P = B
    for _ in range(sh - 1):
        P = _bmm(P, P)
        T = T + _bmm(T, P)
    while sh < int(math.log2(C)):
        off = ((r >> (sh + 1)) == (c >> (sh + 1))) & ((r >> sh) != (c >> sh))
        T = T - _bmm(_bmm(T, jnp.where(off, A, 0.0)), T)
        sh += 1
    return T


def _gdn_chunk(d, q_ref, k_ref, v_ref, gb, Gc_all, Gr_all, S_ref, o_ref, incl, strict, eye):
    C, H, dk = GDN_CHUNK, GDN_HEADS, GDN_HEAD_DIM
    last = C - 1 if d == 0 else 0
    cols = [d * H + h for h in range(H)]
    heads = lambda f: jnp.stack([f(h, cols[h]) for h in range(H)])
    Gc = heads(lambda h, col: Gc_all[:, col:col + 1])
    Gr = heads(lambda h, col: Gr_all[col:col + 1, :])
    Gtot = heads(lambda h, col: Gc_all[last:last + 1, col:col + 1])
    beta = heads(lambda h, col: gb[:, 2 * H + col:2 * H + col + 1])
    q = heads(lambda h, col: q_ref[0, :, h * dk:(h + 1) * dk])
    k = heads(lambda h, col: k_ref[0, :, h * dk:(h + 1) * dk])
    v = heads(lambda h, col: v_ref[0, :, h * dk:(h + 1) * dk])
    decay = jnp.where(incl, jnp.exp(jnp.minimum(Gc - Gr, 0.0)), 0.0)
    kb = k * beta
    kq = jnp.einsum("hid,hjd->hij", jnp.concatenate([kb, q], axis=1).astype(BF16), k.astype(BF16),
                    preferred_element_type=F32)
    A = jnp.where(strict, kq[:, :C] * decay, 0.0)
    attn = kq[:, C:] * decay
    T = _unit_triangular_inverse(A, eye)
    eG = jnp.exp(Gc)
    wu = _bmm(T, jnp.concatenate([kb * eG, v * beta], axis=2))
    S = S_ref[d]
    wq = _bmm(jnp.concatenate([wu[:, :, :dk], q * eG], axis=1), S)
    v_new = wu[:, :, dk:] - wq[:, :C]
    o = wq[:, C:] + _bmm(attn, v_new)
    k_tail = k * jnp.exp(Gtot - Gc)
    S_ref[d] = S * jnp.exp(Gtot) + jnp.einsum("hcd,hce->hde", k_tail.astype(BF16), v_new.astype(BF16),
                                              preferred_element_type=F32)
    for h in range(H):
        o_ref[0, :, h * dk:(h + 1) * dk] = o[h]


def _gdn_scan_kernel(qf_ref, kf_ref, vf_ref, gbf_ref, gtf_ref, qb_ref, kb_ref, vb_ref, gbb_ref, gtb_ref,
                     s0_ref, of_ref, ob_ref, sout_ref, S_ref, *, nsteps):
    i = pl.program_id(1)

    @pl.when(i == 0)
    def _():
        S_ref[...] = s0_ref[0]

    C = GDN_CHUNK
    r = lax.broadcasted_iota(I32, (C, C), 0)
    c = lax.broadcasted_iota(I32, (C, C), 1)
    eye = jnp.where(r == c, 1.0, 0.0).astype(F32)
    streams = ((qf_ref, kf_ref, vf_ref, gbf_ref, gtf_ref, of_ref, r >= c, r > c),
               (qb_ref, kb_ref, vb_ref, gbb_ref, gtb_ref, ob_ref, r <= c, r < c))
    for d, (q_ref, k_ref, v_ref, gb_ref, gt_ref, o_ref, incl, strict) in enumerate(streams):
        tri = jnp.where(incl, 1.0, 0.0).astype(F32)
        gb = gb_ref[0]
        Gc_all = jnp.dot(tri, gb, preferred_element_type=F32, precision=HIGHEST)
        Gr_all = lax.dot_general(gt_ref[0], tri, _NT, preferred_element_type=F32, precision=HIGHEST)
        _gdn_chunk(d, q_ref, k_ref, v_ref, gb, Gc_all, Gr_all, S_ref, o_ref, incl, strict, eye)

    @pl.when(i == nsteps - 1)
    def _():
        sout_ref[0] = S_ref[...]


def gdn_scan(qkv, gb, gbt, s0):
    B, T, D3 = qkv.shape
    D = D3 // 3
    C = GDN_CHUNK
    n = T // C
    W = gb.shape[2]
    R = gbt.shape[1]
    fwd = lambda j: (lambda b, i: (b, i, j))
    bwd = lambda j: (lambda b, i: (b, n - 1 - i, j))
    st = lambda b, i: (b, 0, 0, 0, 0)
    sspec = pl.BlockSpec((1, 2, GDN_HEADS, GDN_HEAD_DIM, GDN_HEAD_DIM), st)
    return pl.pallas_call(
        functools.partial(_gdn_scan_kernel, nsteps=n),
        grid=(B, n),
        in_specs=[pl.BlockSpec((1, C, D), fwd(0)), pl.BlockSpec((1, C, D), fwd(1)), pl.BlockSpec((1, C, D), fwd(2)),
                  pl.BlockSpec((1, C, W), fwd(0)), pl.BlockSpec((1, R, C), lambda b, i: (b, 0, i)),
                  pl.BlockSpec((1, C, D), bwd(0)), pl.BlockSpec((1, C, D), bwd(1)), pl.BlockSpec((1, C, D), bwd(2)),
                  pl.BlockSpec((1, C, W), bwd(0)), pl.BlockSpec((1, R, C), lambda b, i: (b, 0, n - 1 - i)),
                  sspec],
        out_specs=[pl.BlockSpec((1, C, D), fwd(0)), pl.BlockSpec((1, C, D), bwd(0)), sspec],
        out_shape=[jax.ShapeDtypeStruct((B, T, D), F32), jax.ShapeDtypeStruct((B, T, D), F32),
                   jax.ShapeDtypeStruct(s0.shape, F32)],
        scratch_shapes=[pltpu.VMEM((2, GDN_HEADS, GDN_HEAD_DIM, GDN_HEAD_DIM), F32)],
        compiler_params=_params(2, VMEM_MID),
        name="gdn_scan",
    )(qkv, qkv, qkv, gb, gbt, qkv, qkv, qkv, gb, gbt, s0)


def _router_kernel(x_ref, sc_ref, sh_ref, wr_ref, o_ref):
    h = x_ref[0] * (1.0 + sc_ref[0]) + sh_ref[0]
    lg = lax.dot_general(wr_ref[...], h, _NT, preferred_element_type=F32, precision=HIGHEST)
    e = jnp.exp(lg - jnp.max(lg, 0, keepdims=True))
    o_ref[0] = e / jnp.sum(e, 0, keepdims=True)


def router(x, sc, sh, w_router_t, tt):
    B, n, D = x.shape
    E = w_router_t.shape[0]
    tt = min(tt, n)
    return pl.pallas_call(
        _router_kernel,
        grid=(B, n // tt),
        in_specs=[pl.BlockSpec((1, tt, D), lambda b, i: (b, i, 0)),
                  pl.BlockSpec((1, 1, D), lambda b, i: (b, 0, 0)),
                  pl.BlockSpec((1, 1, D), lambda b, i: (b, 0, 0)),
                  pl.BlockSpec((E, D), lambda b, i: (0, 0))],
        out_specs=pl.BlockSpec((1, E, tt), lambda b, i: (b, 0, i)),
        out_shape=jax.ShapeDtypeStruct((B, E, n), F32),
        compiler_params=_params(2, VMEM_MID),
        name="router",
    )(x, sc, sh, w_router_t)


def _exclusive_cumsum_lanes(mask, n):
    r = lax.broadcasted_iota(I32, (LANES, LANES), 0)
    c = lax.broadcasted_iota(I32, (LANES, LANES), 1)
    upper = jnp.where(r < c, 1.0, 0.0).astype(BF16)
    carry = jnp.zeros((mask.shape[0], 1), F32)
    out = []
    for j in range(n // LANES):
        blk = mask[:, j * LANES:(j + 1) * LANES]
        out.append(jnp.dot(blk.astype(BF16), upper, preferred_element_type=F32) + carry)
        carry = carry + jnp.sum(blk, -1, keepdims=True)
    return jnp.concatenate(out, axis=1)


def _topk_kernel(aff_ref, idx_ref, gate_ref, pos_ref, lhs_ref, res_ref, *, n, cap, tb):
    E = aff_ref.shape[1]
    aff = aff_ref[0]
    bits = pltpu.bitcast(aff, I32)
    count = lambda m: jnp.sum(jnp.where(m, 1.0, 0.0), -1, keepdims=True)
    t = jnp.zeros((E, 1), I32)
    for bit in range(30, -1, -1):
        cand = t | (1 << bit)
        t = jnp.where(count(bits >= cand) >= cap, cand, t)
    gt = bits > t
    eq = bits == t
    need = cap - count(gt)
    eq_rank = _exclusive_cumsum_lanes(jnp.where(eq, 1.0, 0.0), n)
    sel = gt | (eq & (eq_rank < need))
    pos = _exclusive_cumsum_lanes(jnp.where(sel, 1.0, 0.0), n)
    pos_ref[...] = jnp.where(sel, pos, -1.0)
    tok = lax.broadcasted_iota(I32, (E, n), 1)
    row = lax.broadcasted_iota(I32, (E, n), 0)
    tokrows = jnp.where(row == 0, tok // 64, jnp.where(row == 1, tok % 64, 0)).astype(F32)
    g_hi = aff.astype(BF16)
    r1 = aff - g_hi.astype(F32)
    g_mid = r1.astype(BF16)
    g_lo = (r1 - g_mid.astype(F32)).astype(BF16)
    lhs_ref[0 * E:1 * E, :] = tokrows.astype(BF16)
    lhs_ref[1 * E:2 * E, :] = g_hi
    lhs_ref[2 * E:3 * E, :] = g_mid
    lhs_ref[3 * E:4 * E, :] = g_lo
    slot = lax.broadcasted_iota(I32, (cap, tb), 0).astype(F32)

    def per_expert(e, carry):
        acc = jnp.zeros((4 * E, cap), F32)
        for j in range(n // tb):
            blk = slice(j * tb, (j + 1) * tb)
            onehot = jnp.where(pos_ref[pl.ds(e, 1), blk] == slot, 1.0, 0.0).astype(BF16)
            acc = acc + lax.dot_general(lhs_ref[:, blk], onehot, _NT, preferred_element_type=F32)
        res_ref[...] = acc
        idx_row = res_ref[0:1, :] * 64.0 + res_ref[1:2, :]
        idx_ref[0, pl.ds(e, 1), :] = idx_row.astype(I32)
        gate_ref[0, pl.ds(e, 1), :] = (res_ref[pl.ds(E + e, 1), :] + res_ref[pl.ds(2 * E + e, 1), :]
                                       + res_ref[pl.ds(3 * E + e, 1), :])
        return carry

    lax.fori_loop(0, E, per_expert, 0)


def topk_select(aff, cap):
    B, E, n = aff.shape
    tb = min(512, n)
    return pl.pallas_call(
        functools.partial(_topk_kernel, n=n, cap=cap, tb=tb),
        grid=(B,),
        in_specs=[pl.BlockSpec((1, E, n), lambda b: (b, 0, 0))],
        out_specs=[pl.BlockSpec((1, E, cap), lambda b: (b, 0, 0)),
                   pl.BlockSpec((1, E, cap), lambda b: (b, 0, 0))],
        out_shape=[jax.ShapeDtypeStruct((B, E, cap), I32), jax.ShapeDtypeStruct((B, E, cap), F32)],
        scratch_shapes=[pltpu.VMEM((E, n), F32), pltpu.VMEM((4 * E, n), BF16), pltpu.VMEM((4 * E, cap), F32)],
        compiler_params=_params(1, VMEM_MID),
        name="topk_select",
    )(aff)


def _moe_kernel(idx_ref, gate_ref, x_ref, sc_ref, sh_ref, wg_ref, wu_ref, wd_ref, y_ref,
                xs32_ref, xs_ref, ye_ref, yp_ref, *, S, nfc):
    e = pl.program_id(0)
    f = pl.program_id(1)
    E = pl.num_programs(0)
    D = x_ref.shape[-1]
    sub = lax.broadcasted_iota(I32, (SUBLANES, D), 0)
    per_step = S // nfc // SUBLANES

    def gather8(expert, base):
        tile = jnp.zeros((SUBLANES, D), F32)
        for j in range(SUBLANES):
            i = idx_ref[expert * S + base + j]
            chunk = x_ref[0, pl.ds(pl.multiple_of((i >> 3) << 3, SUBLANES), SUBLANES), :]
            tile = jnp.where(sub == j, pltpu.roll(chunk, (j - (i & 7)) & 7, 0), tile)
        xs32_ref[pl.ds(base, SUBLANES), :] = tile

    def scatter8(expert, base, scale):
        ye8 = yp_ref[pl.ds(base, SUBLANES), :]
        for j in range(SUBLANES):
            i = idx_ref[expert * S + base + j]
            g = gate_ref[expert * S + base + j] * scale
            rows = pl.ds(pl.multiple_of((i >> 3) << 3, SUBLANES), SUBLANES)
            contrib = jnp.where(sub == (i & 7), pltpu.roll(ye8, ((i & 7) - j) & 7, 0) * g, 0.0)
            y_ref[0, rows, :] = y_ref[0, rows, :] + contrib

    def modulated_bf16():
        return (xs32_ref[...] * (1.0 + sc_ref[0]) + sh_ref[0]).astype(BF16)

    @pl.when((e == 0) & (f == 0))
    def _():
        y_ref[0] = jnp.zeros(y_ref.shape[1:], F32)
        yp_ref[...] = jnp.zeros(yp_ref.shape, F32)

        def body(s8, carry):
            gather8(0, pl.multiple_of(s8 * SUBLANES, SUBLANES))
            return carry

        lax.fori_loop(0, S // SUBLANES, body, 0)
        xs_ref[...] = modulated_bf16()

    prev = jnp.maximum(e - 1, 0)
    prev_scale = jnp.where(e > 0, 1.0, 0.0)
    nxt = jnp.minimum(e + 1, E - 1)
    for s8 in range(per_step):
        base = pl.multiple_of((f * per_step + s8) * SUBLANES, SUBLANES)
        scatter8(prev, base, prev_scale)
        gather8(nxt, base)

    xs = xs_ref[...]
    a = jnp.dot(xs, wg_ref[0], preferred_element_type=F32)
    u = jnp.dot(xs, wu_ref[0], preferred_element_type=F32)
    part = jnp.dot((_silu(a) * u).astype(BF16), wd_ref[0], preferred_element_type=F32)

    @pl.when(f == 0)
    def _():
        ye_ref[...] = part

    @pl.when(f > 0)
    def _():
        ye_ref[...] += part

    @pl.when(f == nfc - 1)
    def _():
        yp_ref[...] = ye_ref[...]
        xs_ref[...] = modulated_bf16()

    @pl.when((e == E - 1) & (f == nfc - 1))
    def _():
        def body(s8, carry):
            scatter8(e, pl.multiple_of(s8 * SUBLANES, SUBLANES), 1.0)
            return carry

        lax.fori_loop(0, S // SUBLANES, body, 0)


def moe_experts(idx, gate, x, g, sc, sh, wg, wu, wd, fc):
    G, R, D = x.shape
    E, _, F = wg.shape
    S = idx.shape[0] // E
    nfc = F // fc
    return pl.pallas_call(
        functools.partial(_moe_kernel, S=S, nfc=nfc),
        grid_spec=pltpu.PrefetchScalarGridSpec(
            num_scalar_prefetch=2,
            grid=(E, nfc),
            in_specs=[pl.BlockSpec((1, R, D), lambda e, f, *_: (g, 0, 0), pipeline_mode=pl.Buffered(1)),
                      pl.BlockSpec((1, 1, D), lambda e, f, *_: (g, 0, 0)),
                      pl.BlockSpec((1, 1, D), lambda e, f, *_: (g, 0, 0)),
                      pl.BlockSpec((1, D, fc), lambda e, f, *_: (e, 0, f)),
                      pl.BlockSpec((1, D, fc), lambda e, f, *_: (e, 0, f)),
                      pl.BlockSpec((1, fc, D), lambda e, f, *_: (e, f, 0))],
            out_specs=pl.BlockSpec((1, R, D), lambda e, f, *_: (0, 0, 0), pipeline_mode=pl.Buffered(1)),
            scratch_shapes=[pltpu.VMEM((S, D), F32), pltpu.VMEM((S, D), BF16), pltpu.VMEM((S, D), F32),
                            pltpu.VMEM((S, D), F32)]),
        out_shape=jax.ShapeDtypeStruct((1, R, D), F32),
        compiler_params=_params(2, VMEM_BIG),
        name="moe_experts",
    )(idx, gate, x, sc, sh, wg, wu, wd)[0]


def ec_moe(x, sc, sh, w_router_t, wg, wu, wd, flatten_groups):
    B, n, D = x.shape
    E = w_router_t.shape[0]
    cap = EC_CAPACITY_FACTOR * n // E
    aff = router(x, sc, sh, w_router_t, 512)
    idx, gate = topk_select(aff, cap)
    fc = 512
    if flatten_groups:
        idx = idx + (jnp.arange(B, dtype=I32) * n)[:, None, None]
        idx = idx.transpose(1, 0, 2).reshape(-1)
        gate = gate.transpose(1, 0, 2).reshape(-1)
        y = moe_experts(idx, gate, x.reshape(1, B * n, D), 0, sc[:1], sh[:1], wg, wu, wd, fc)
        return y.reshape(B, n, D)
    ys = [moe_experts(idx[b].reshape(-1), gate[b].reshape(-1), x, b, sc, sh, wg, wu, wd, fc) for b in range(B)]
    return jnp.stack(ys)


def kernel(x, c, ctx, c_ctx, ada_w, ada_b, ln_g, ln_b, na_w_qkv, na_w_o, na_rpb, gdn_w_in, gdn_conv_w,
           gdn_a_log, gdn_dt_bias, gdn_norm_w, gdn_w_o, moe_w_router, moe_w_gate, moe_w_up, moe_w_down):
    B, N, D = x.shape
    L = ctx.shape[1]
    depth = ada_w.shape[0]
    alpha = (2.0 * depth) ** 0.25
    xc = ctx

    rows = -(-(B + 1) // SUBLANES) * SUBLANES
    cs = jnp.zeros((rows, D), F32).at[:B].set(c).at[B].set(c_ctx)
    mods = ada_modulation(cs, ada_w, ada_b)

    wg_all = moe_w_gate.astype(BF16)
    wu_all = moe_w_up.astype(BF16)
    wd_all = moe_w_down.astype(BF16)

    for l in range(depth):
        last = l == depth - 1
        i = l // 2
        mod = [mods[l, :B, j * D:(j + 1) * D].reshape(B, 1, D) for j in range(6)]
        modc = [jnp.broadcast_to(mods[l, B, j * D:(j + 1) * D].reshape(1, 1, D), (B, 1, D)) for j in range(6)]
        g1, b1, g2, b2 = ln_g[l, 0], ln_b[l, 0], ln_g[l, 1], ln_b[l, 1]
        if l % 2 == 0:
            qkv = mm_mod(x, mod[1], mod[0], na_w_qkv[i], BF16, 1024, 1024)
            qkvc = mm_mod(xc, modc[1], modc[0], na_w_qkv[i], BF16, 1024, 1024)
            o = natten(qkv, qkvc, _natten_bias_table(na_rpb[i]))
            x = mm_ln(o, x, mod[2], na_w_o[i], g1, b1, alpha, 512)
            if not last:
                xc = mm_ln(ctx_attention(qkvc), xc, modc[2], na_w_o[i], g1, b1, alpha, 512)
        else:
            w_main = gdn_w_in[i][:, :4 * D]
            w_ab = jnp.pad(gdn_w_in[i][:, 4 * D:], ((0, 0), (0, LANES - 4 * GDN_HEADS)))

            def project(u, m):
                p = mm_mod(u, m[1], m[0], w_main, F32, 1024, 1024)
                ab = mm_mod(u, m[1], m[0], w_ab, F32, 1024, LANES)
                gb = gdn_gates(ab, gdn_a_log[i], gdn_dt_bias[i], 1024)
                gbt = jnp.swapaxes(gb[:, :, :4 * GDN_HEADS], 1, 2)
                return p, gdn_conv(p, gdn_conv_w[i], 512), gb, gbt

            p, qkv, gb, gbt = project(x, mod)
            pc, qkvc, gbc, gbtc = project(xc, modc)
            s0 = jnp.zeros((B, 2, GDN_HEADS, GDN_HEAD_DIM, GDN_HEAD_DIM), F32)
            oc_f, oc_b, s_ctx = gdn_scan(qkvc, gbc, gbtc, s0)
            o_f, o_b, _ = gdn_scan(qkv, gb, gbt, s_ctx)
            x = mm_ln_gdn(o_f, o_b, p, gdn_norm_w[i], x, mod[2], gdn_w_o[i], g1, b1, alpha, 512)
            if not last:
                xc = mm_ln_gdn(oc_f, oc_b, pc, gdn_norm_w[i], xc, modc[2], gdn_w_o[i], g1, b1, alpha, 512)
        wr_t = moe_w_router[l].T
        y = ec_moe(x, mod[4], mod[3], wr_t, wg_all[l], wu_all[l], wd_all[l], False)
        x = res_ln(y, x, mod[5], g2, b2, alpha, 512)
        if not last:
            yc = ec_moe(xc, modc[4], modc[3], wr_t, wg_all[l], wu_all[l], wd_all[l], True)
            xc = res_ln(yc, xc, modc[5], g2, b2, alpha, 512)
    return x
```

```python
import functools
import math

import numpy as np
import jax
import jax.numpy as jnp
from jax import lax
from jax.experimental import pallas as pl
from jax.experimental.pallas import tpu as pltpu

F32 = jnp.float32
BF16 = jnp.bfloat16
I32 = jnp.int32
HIGHEST = lax.Precision.HIGHEST

GRID_W = 64
NA_HEADS = 16
NA_HEAD_DIM = 64
NA_KR = 8
NA_KC = 16
GDN_HEADS = 8
GDN_HEAD_DIM = 128
GDN_CONV = 5
N_EXPERTS = 16
EC_CAPACITY_FACTOR = 2
LN_EPS = 1e-6
NEG_INF = -1e30

GDN_CHUNK = 128
CONV_HALO = 8
SUBLANES = 8
LANES = 128
MOE_HIDDEN_CHUNK = 1024
V7X_VMEM_BYTES = 64 * 1024 * 1024
VMEM_BIG = 56 * 1024 * 1024
VMEM_MID = 40 * 1024 * 1024

_NT = (((1,), (1,)), ((), ()))
_TN = (((0,), (0,)), ((), ()))


def _params(n_axes, vmem):
    return pltpu.CompilerParams(dimension_semantics=("arbitrary",) * n_axes, vmem_limit_bytes=vmem)


def _silu(x):
    return x * jax.nn.sigmoid(x)


def _layernorm_rows(z, g, b):
    mu = jnp.mean(z, -1, keepdims=True)
    zc = z - mu
    var = jnp.mean(zc * zc, -1, keepdims=True)
    return zc * lax.rsqrt(var + LN_EPS) * g + b


def _ada_kernel(cs_ref, w_ref, b_ref, o_ref):
    s = _silu(cs_ref[...])
    o_ref[0] = jnp.dot(s, w_ref[0], preferred_element_type=F32, precision=HIGHEST) + b_ref[0]


def ada_modulation(cs, ada_w, ada_b):
    depth, D, D6 = ada_w.shape
    R = cs.shape[0]
    tn = D6 // 4
    return pl.pallas_call(
        _ada_kernel,
        grid=(depth, D6 // tn),
        in_specs=[pl.BlockSpec((R, D), lambda l, j: (0, 0)),
                  pl.BlockSpec((1, D, tn), lambda l, j: (l, 0, j)),
                  pl.BlockSpec((1, 1, tn), lambda l, j: (l, 0, j))],
        out_specs=pl.BlockSpec((1, R, tn), lambda l, j: (l, 0, j)),
        out_shape=jax.ShapeDtypeStruct((depth, R, D6), F32),
        compiler_params=_params(2, VMEM_MID),
        name="ada_modulation",
    )(cs, ada_w, ada_b.reshape(depth, 1, D6))


def _mm_mod_kernel(x_ref, sc_ref, sh_ref, w_ref, o_ref, wb_ref):
    @pl.when((pl.program_id(1) == 0) & (pl.program_id(2) == 0))
    def _():
        wb_ref[...] = w_ref[...].astype(BF16)

    h = x_ref[0] * (1.0 + sc_ref[0]) + sh_ref[0]
    o_ref[0] = jnp.dot(h.astype(BF16), wb_ref[...], preferred_element_type=F32).astype(o_ref.dtype)


def mm_mod(x, sc, sh, w, out_dtype, tm, tn):
    B, N, D = x.shape
    NO = w.shape[1]
    tm = min(tm, N)
    tn = min(tn, NO)
    return pl.pallas_call(
        _mm_mod_kernel,
        grid=(NO // tn, B, N // tm),
        in_specs=[pl.BlockSpec((1, tm, D), lambda j, b, i: (b, i, 0)),
                  pl.BlockSpec((1, 1, D), lambda j, b, i: (b, 0, 0)),
                  pl.BlockSpec((1, 1, D), lambda j, b, i: (b, 0, 0)),
                  pl.BlockSpec((D, tn), lambda j, b, i: (0, j))],
        out_specs=pl.BlockSpec((1, tm, tn), lambda j, b, i: (b, i, j)),
        out_shape=jax.ShapeDtypeStruct((B, N, NO), out_dtype),
        scratch_shapes=[pltpu.VMEM((D, tn), BF16)],
        compiler_params=_params(3, VMEM_MID),
        name="mm_mod",
    )(x, sc, sh, w)


def _natten_bias_table(rpb):
    H = rpb.shape[0]
    qc = np.arange(GRID_W)
    kc = np.arange(GRID_W)
    col_start = np.clip(qc - NA_KC // 2, 0, GRID_W - NA_KC)
    valid = (kc[None, :] >= col_start[:, None]) & (kc[None, :] < col_start[:, None] + NA_KC)
    dc = np.clip(kc[None, :] - qc[:, None], -(NA_KC - 1), NA_KC - 1) + NA_KC - 1
    tab = rpb.astype(F32)[:, :, dc] + jnp.where(jnp.asarray(valid), 0.0, NEG_INF).astype(F32)[None, None]
    variants = [tab[:, d0:d0 + NA_KR].transpose(0, 2, 1, 3).reshape(H, GRID_W, NA_KR * GRID_W)
                for d0 in range(NA_KR)]
    return jnp.stack(variants)


def _softmax_pv(s_list, v_list):
    m = functools.reduce(jnp.maximum, [jnp.max(s, -1, keepdims=True) for s in s_list])
    ps = [jnp.exp(s - m) for s in s_list]
    den = functools.reduce(lambda a, b: a + b, [jnp.sum(p, -1, keepdims=True) for p in ps])
    o = functools.reduce(lambda a, b: a + b,
                         [jnp.dot(p.astype(BF16), v, preferred_element_type=F32) for p, v in zip(ps, v_list)])
    return o / den


def _natten_kernel(q_ref, k_ref, v_ref, kc_ref, vc_ref, bias_ref, o_ref, *, rows, scale):
    r = pl.program_id(1)
    rs = jnp.clip(r - NA_KR // 2, 0, rows - NA_KR)
    start = pl.multiple_of(rs * GRID_W, GRID_W)
    win = pl.ds(start, NA_KR * GRID_W)
    lane = lax.broadcasted_iota(I32, (GRID_W, LANES), 1)
    per_tile = LANES // NA_HEAD_DIM
    masks = [(lane >= half * NA_HEAD_DIM) & (lane < (half + 1) * NA_HEAD_DIM) for half in range(per_tile)]
    tiles = [slice(pair * LANES, (pair + 1) * LANES) for pair in range(NA_HEADS // per_tile)]
    scores = []
    for pair, sl in enumerate(tiles):
        q2 = q_ref[0, :, sl] * scale
        for half in range(per_tile):
            qh = jnp.where(masks[half], q2, jnp.zeros_like(q2))
            s_w = lax.dot_general(qh, k_ref[0, win, sl], _NT, preferred_element_type=F32)
            s_c = lax.dot_general(qh, kc_ref[0, :, sl], _NT, preferred_element_type=F32)
            scores.append((s_w + bias_ref[0, pair * per_tile + half], s_c))
    maxes = [jnp.maximum(jnp.max(s_w, -1, keepdims=True), jnp.max(s_c, -1, keepdims=True)) for s_w, s_c in scores]
    probs = [(jnp.exp(s_w - m), jnp.exp(s_c - m)) for (s_w, s_c), m in zip(scores, maxes)]
    dens = [jnp.sum(p_w, -1, keepdims=True) + jnp.sum(p_c, -1, keepdims=True) for p_w, p_c in probs]
    for pair, sl in enumerate(tiles):
        o = None
        for half in range(per_tile):
            h = pair * per_tile + half
            p_w, p_c = probs[h]
            pv = (jnp.dot(p_w.astype(BF16), v_ref[0, win, sl], preferred_element_type=F32)
                  + jnp.dot(p_c.astype(BF16), vc_ref[0, :, sl], preferred_element_type=F32)) / dens[h]
            o = pv if o is None else jnp.where(masks[half], pv, o)
        o_ref[0, :, sl] = o.astype(o_ref.dtype)


def natten(qkv, qkvc, bias_tab):
    B, N, D3 = qkv.shape
    D = D3 // 3
    L = qkvc.shape[1]
    rows = N // GRID_W
    assert rows >= NA_KR and N % GRID_W == 0
    scale = NA_HEAD_DIM ** -0.5
    assert math.frexp(scale)[0] == 0.5, "q is pre-scaled in bf16: the scale must be a power of two"

    def bias_index(b, r):
        rs = jnp.clip(r - NA_KR // 2, 0, rows - NA_KR)
        return (rs - r + NA_KR - 1, 0, 0, 0)

    return pl.pallas_call(
        functools.partial(_natten_kernel, rows=rows, scale=scale),
        grid=(B, rows),
        in_specs=[pl.BlockSpec((1, GRID_W, D), lambda b, r: (b, r, 0)),
                  pl.BlockSpec((1, N, D), lambda b, r: (b, 0, 1)),
                  pl.BlockSpec((1, N, D), lambda b, r: (b, 0, 2)),
                  pl.BlockSpec((1, L, D), lambda b, r: (b, 0, 1)),
                  pl.BlockSpec((1, L, D), lambda b, r: (b, 0, 2)),
                  pl.BlockSpec((1, NA_HEADS, GRID_W, NA_KR * GRID_W), bias_index)],
        out_specs=pl.BlockSpec((1, GRID_W, D), lambda b, r: (b, r, 0)),
        out_shape=jax.ShapeDtypeStruct((B, N, D), BF16),
        compiler_params=_params(2, VMEM_BIG),
        name="natten",
    )(qkv, qkv, qkv, qkvc, qkvc, bias_tab)


def _ctx_attn_kernel(q_ref, k_ref, v_ref, o_ref, *, scale):
    for h in range(NA_HEADS):
        sl = slice(h * NA_HEAD_DIM, (h + 1) * NA_HEAD_DIM)
        s = lax.dot_general(q_ref[0, :, sl], k_ref[0, :, sl], _NT, preferred_element_type=F32) * scale
        o_ref[0, :, sl] = _softmax_pv([s], [v_ref[0, :, sl]]).astype(o_ref.dtype)


def ctx_attention(qkvc):
    B, L, D3 = qkvc.shape
    D = D3 // 3
    return pl.pallas_call(
        functools.partial(_ctx_attn_kernel, scale=NA_HEAD_DIM ** -0.5),
        grid=(B,),
        in_specs=[pl.BlockSpec((1, L, D), lambda b: (b, 0, 0)),
                  pl.BlockSpec((1, L, D), lambda b: (b, 0, 1)),
                  pl.BlockSpec((1, L, D), lambda b: (b, 0, 2))],
        out_specs=pl.BlockSpec((1, L, D), lambda b: (b, 0, 0)),
        out_shape=jax.ShapeDtypeStruct((B, L, D), BF16),
        compiler_params=_params(1, VMEM_MID),
        name="ctx_attention",
    )(qkvc, qkvc, qkvc)


def _is_first_step():
    return (pl.program_id(0) == 0) & (pl.program_id(1) == 0)


def _mm_ln_kernel(a_ref, x_ref, gate_ref, w_ref, g_ref, b_ref, o_ref, wb_ref, *, alpha):
    @pl.when(_is_first_step())
    def _():
        wb_ref[...] = w_ref[...].astype(BF16)

    y = jnp.dot(a_ref[0], wb_ref[...], preferred_element_type=F32)
    o_ref[0] = _layernorm_rows(alpha * x_ref[0] + gate_ref[0] * y, g_ref[...], b_ref[...])


def mm_ln(a, x, gate, w, g, b, alpha, tm):
    B, N, D = x.shape
    tm = min(tm, N)
    row = lambda bi, i: (bi, i, 0)
    vec = lambda bi, i: (0, 0)
    return pl.pallas_call(
        functools.partial(_mm_ln_kernel, alpha=alpha),
        grid=(B, N // tm),
        in_specs=[pl.BlockSpec((1, tm, D), row),
                  pl.BlockSpec((1, tm, D), row),
                  pl.BlockSpec((1, 1, D), lambda bi, i: (bi, 0, 0)),
                  pl.BlockSpec((D, D), vec),
                  pl.BlockSpec((1, D), vec),
                  pl.BlockSpec((1, D), vec)],
        out_specs=pl.BlockSpec((1, tm, D), row),
        out_shape=jax.ShapeDtypeStruct((B, N, D), F32),
        scratch_shapes=[pltpu.VMEM((D, D), BF16)],
        compiler_params=_params(2, VMEM_MID),
        name="mm_ln",
    )(a, x, gate, w, g.reshape(1, D), b.reshape(1, D))


def _mm_ln_gdn_kernel(of_ref, ob_ref, z_ref, nw_ref, x_ref, gate_ref, w_ref, g_ref, b_ref, o_ref,
                      wb_ref, a_ref, *, alpha):
    @pl.when(_is_first_step())
    def _():
        wb_ref[...] = w_ref[...].astype(BF16)

    for h in range(GDN_HEADS):
        sl = slice(h * GDN_HEAD_DIM, (h + 1) * GDN_HEAD_DIM)
        o = of_ref[0, :, sl] + ob_ref[0, :, sl]
        y = o * lax.rsqrt(jnp.mean(o * o, -1, keepdims=True) + LN_EPS) * nw_ref[...]
        a_ref[:, sl] = (y * _silu(z_ref[0, :, sl])).astype(BF16)
    y = jnp.dot(a_ref[...], wb_ref[...], preferred_element_type=F32)
    o_ref[0] = _layernorm_rows(alpha * x_ref[0] + gate_ref[0] * y, g_ref[...], b_ref[...])


def mm_ln_gdn(o_f, o_b, p, norm_w, x, gate, w, g, b, alpha, tm):
    B, N, D = x.shape
    tm = min(tm, N)
    row = lambda bi, i: (bi, i, 0)
    vec = lambda bi, i: (0, 0)
    return pl.pallas_call(
        functools.partial(_mm_ln_gdn_kernel, alpha=alpha),
        grid=(B, N // tm),
        in_specs=[pl.BlockSpec((1, tm, D), row),
                  pl.BlockSpec((1, tm, D), row),
                  pl.BlockSpec((1, tm, D), lambda bi, i: (bi, i, 3)),
                  pl.BlockSpec((1, GDN_HEAD_DIM), vec),
                  pl.BlockSpec((1, tm, D), row),
                  pl.BlockSpec((1, 1, D), lambda bi, i: (bi, 0, 0)),
                  pl.BlockSpec((D, D), vec),
                  pl.BlockSpec((1, D), vec),
                  pl.BlockSpec((1, D), vec)],
        out_specs=pl.BlockSpec((1, tm, D), row),
        out_shape=jax.ShapeDtypeStruct((B, N, D), F32),
        scratch_shapes=[pltpu.VMEM((D, D), BF16), pltpu.VMEM((tm, D), BF16)],
        compiler_params=_params(2, VMEM_MID),
        name="mm_ln_gdn",
    )(o_f, o_b, p, norm_w.reshape(1, GDN_HEAD_DIM), x, gate, w, g.reshape(1, D), b.reshape(1, D))


def _res_ln_kernel(y_ref, x_ref, gate_ref, g_ref, b_ref, o_ref, *, alpha):
    o_ref[0] = _layernorm_rows(alpha * x_ref[0] + gate_ref[0] * y_ref[0], g_ref[...], b_ref[...])


def res_ln(y, x, gate, g, b, alpha, tm):
    B, N, D = x.shape
    tm = min(tm, N)
    row = lambda bi, i: (bi, i, 0)
    vec = lambda bi, i: (0, 0)
    return pl.pallas_call(
        functools.partial(_res_ln_kernel, alpha=alpha),
        grid=(B, N // tm),
        in_specs=[pl.BlockSpec((1, tm, D), row),
                  pl.BlockSpec((1, tm, D), row),
                  pl.BlockSpec((1, 1, D), lambda bi, i: (bi, 0, 0)),
                  pl.BlockSpec((1, D), vec),
                  pl.BlockSpec((1, D), vec)],
        out_specs=pl.BlockSpec((1, tm, D), row),
        out_shape=jax.ShapeDtypeStruct((B, N, D), F32),
        compiler_params=_params(2, VMEM_MID),
        name="res_ln",
    )(y, x, gate, g.reshape(1, D), b.reshape(1, D))


def _conv_kernel(prev_ref, main_ref, next_ref, w_ref, o_ref, ext_ref, *, tt, nt):
    i = pl.program_id(1)
    j = pl.program_id(2)
    ext_ref[0:CONV_HALO, :] = jnp.where(i > 0, prev_ref[0], 0.0)
    ext_ref[CONV_HALO:CONV_HALO + tt, :] = main_ref[0]
    ext_ref[CONV_HALO + tt:, :] = jnp.where(i < nt - 1, next_ref[0], 0.0)
    pad = GDN_CONV // 2
    acc = w_ref[0:1, :] * ext_ref[CONV_HALO - pad:CONV_HALO - pad + tt, :]
    for t in range(1, GDN_CONV):
        acc = acc + w_ref[t:t + 1, :] * ext_ref[CONV_HALO - pad + t:CONV_HALO - pad + t + tt, :]
    y = _silu(acc)
    qscale = jnp.where(j == 0, GDN_HEAD_DIM ** -0.5, 1.0)
    for h in range(GDN_HEADS):
        sl = slice(h * GDN_HEAD_DIM, (h + 1) * GDN_HEAD_DIM)
        seg = y[:, sl]
        inv = lax.rsqrt(jnp.sum(seg * seg, -1, keepdims=True) + 1e-6) * qscale
        o_ref[0, :, sl] = seg * jnp.where(j < 2, inv, 1.0)


def gdn_conv(p, conv_w, tt):
    B, T = p.shape[:2]
    D = conv_w.shape[1] // 3
    tt = min(tt, T)
    nt = T // tt
    hb = tt // CONV_HALO
    return pl.pallas_call(
        functools.partial(_conv_kernel, tt=tt, nt=nt),
        grid=(B, nt, 3),
        in_specs=[pl.BlockSpec((1, CONV_HALO, D), lambda b, i, j: (b, jnp.maximum(i * hb - 1, 0), j)),
                  pl.BlockSpec((1, tt, D), lambda b, i, j: (b, i, j)),
                  pl.BlockSpec((1, CONV_HALO, D), lambda b, i, j: (b, jnp.minimum((i + 1) * hb, T // CONV_HALO - 1), j)),
                  pl.BlockSpec((GDN_CONV, D), lambda b, i, j: (0, j))],
        out_specs=pl.BlockSpec((1, tt, D), lambda b, i, j: (b, i, j)),
        out_shape=jax.ShapeDtypeStruct((B, T, 3 * D), F32),
        scratch_shapes=[pltpu.VMEM((tt + 2 * CONV_HALO, D), F32)],
        compiler_params=_params(3, VMEM_MID),
        name="gdn_conv",
    )(p, p, p, conv_w)


def _gates_kernel(ab_ref, alog_ref, dtb_ref, o_ref):
    ab = ab_ref[0]
    xg = ab + dtb_ref[...]
    softplus = jnp.maximum(xg, 0.0) + jnp.log(1.0 + jnp.exp(-jnp.abs(xg)))
    g = -jnp.exp(alog_ref[...]) * softplus
    beta = jax.nn.sigmoid(ab)
    lane = lax.broadcasted_iota(I32, ab.shape, 1)
    o_ref[0] = jnp.where(lane < 2 * GDN_HEADS, g, beta)


def gdn_gates(ab, a_log, dt_bias, tt):
    B, T, W = ab.shape
    tt = min(tt, T)
    pad = lambda a: jnp.pad(a.reshape(1, -1).astype(F32), ((0, 0), (0, W - a.size)))
    return pl.pallas_call(
        _gates_kernel,
        grid=(B, T // tt),
        in_specs=[pl.BlockSpec((1, tt, W), lambda b, i: (b, i, 0)),
                  pl.BlockSpec((1, W), lambda b, i: (0, 0)),
                  pl.BlockSpec((1, W), lambda b, i: (0, 0))],
        out_specs=pl.BlockSpec((1, tt, W), lambda b, i: (b, i, 0)),
        out_shape=jax.ShapeDtypeStruct((B, T, W), F32),
        compiler_params=_params(2, VMEM_MID),
        name="gdn_gates",
    )(ab, pad(a_log), pad(dt_bias))


TRI_BASE = 16


def _bmm(a, b):
    return jnp.einsum("hij,hjk->hik", a.astype(BF16), b.astype(BF16), preferred_element_type=F32)


def _unit_triangular_inverse(A, eye):
    C = A.shape[-1]
    r = lax.broadcasted_iota(I32, (C, C), 0)
    c = lax.broadcasted_iota(I32, (C, C), 1)
    sh = int(math.log2(TRI_BASE))
    B = jnp.where((r >> sh) == (c >> sh), A, 0.0)
    T = eye - B
    P = B
    for _ in range(sh - 1):
        P = _bmm(P, P)
        T = T + _bmm(T, P)
    while sh < int(math.log2(C)):
        off = ((r >> (sh + 1)) == (c >> (sh + 1))) & ((r >> sh) != (c >> sh))
        T = T - _bmm(_bmm(T, jnp.where(off, A, 0.0)), T)
        sh += 1
    return T


def _gdn_chunk(d, q_ref, k_ref, v_ref, gb, Gc_all, Gr_all, S_ref, o_ref, incl, strict, eye):
    C, H, dk = GDN_CHUNK, GDN_HEADS, GDN_HEAD_DIM
    last = C - 1 if d == 0 else 0
    cols = [d * H + h for h in range(H)]
    heads = lambda f: jnp.stack([f(h, cols[h]) for h in range(H)])
    Gc = heads(lambda h, col: Gc_all[:, col:col + 1])
    Gr = heads(lambda h, col: Gr_all[col:col + 1, :])
    Gtot = heads(lambda h, col: Gc_all[last:last + 1, col:col + 1])
    beta = heads(lambda h, col: gb[:, 2 * H + col:2 * H + col + 1])
    q = heads(lambda h, col: q_ref[0, :, h * dk:(h + 1) * dk])
    k = heads(lambda h, col: k_ref[0, :, h * dk:(h + 1) * dk])
    v = heads(lambda h, col: v_ref[0, :, h * dk:(h + 1) * dk])
    decay = jnp.where(incl, jnp.exp(jnp.minimum(Gc - Gr, 0.0)), 0.0)
    kb = k * beta
    kq = jnp.einsum("hid,hjd->hij", jnp.concatenate([kb, q], axis=1).astype(BF16), k.astype(BF16),
                    preferred_element_type=F32)
    A = jnp.where(strict, kq[:, :C] * decay, 0.0)
    attn = kq[:, C:] * decay
    T = _unit_triangular_inverse(A, eye)
    eG = jnp.exp(Gc)
    wu = _bmm(T, jnp.concatenate([kb * eG, v * beta], axis=2))
    S = S_ref[d]
    wq = _bmm(jnp.concatenate([wu[:, :, :dk], q * eG], axis=1), S)
    v_new = wu[:, :, dk:] - wq[:, :C]
    o = wq[:, C:] + _bmm(attn, v_new)
    k_tail = k * jnp.exp(Gtot - Gc)
    S_ref[d] = S * jnp.exp(Gtot) + jnp.einsum("hcd,hce->hde", k_tail.astype(BF16), v_new.astype(BF16),
                                              preferred_element_type=F32)
    for h in range(H):
        o_ref[0, :, h * dk:(h + 1) * dk] = o[h]


def _gdn_scan_kernel(qf_ref, kf_ref, vf_ref, gbf_ref, gtf_ref, qb_ref, kb_ref, vb_ref, gbb_ref, gtb_ref,
                     s0_ref, of_ref, ob_ref, sout_ref, S_ref, *, nsteps):
    i = pl.program_id(1)

    @pl.when(i == 0)
    def _():
        S_ref[...] = s0_ref[0]

    C = GDN_CHUNK
    r = lax.broadcasted_iota(I32, (C, C), 0)
    c = lax.broadcasted_iota(I32, (C, C), 1)
    eye = jnp.where(r == c, 1.0, 0.0).astype(F32)
    streams = ((qf_ref, kf_ref, vf_ref, gbf_ref, gtf_ref, of_ref, r >= c, r > c),
               (qb_ref, kb_ref, vb_ref, gbb_ref, gtb_ref, ob_ref, r <= c, r < c))
    for d, (q_ref, k_ref, v_ref, gb_ref, gt_ref, o_ref, incl, strict) in enumerate(streams):
        tri = jnp.where(incl, 1.0, 0.0).astype(F32)
        gb = gb_ref[0]
        Gc_all = jnp.dot(tri, gb, preferred_element_type=F32, precision=HIGHEST)
        Gr_all = lax.dot_general(gt_ref[0], tri, _NT, preferred_element_type=F32, precision=HIGHEST)
        _gdn_chunk(d, q_ref, k_ref, v_ref, gb, Gc_all, Gr_all, S_ref, o_ref, incl, strict, eye)

    @pl.when(i == nsteps - 1)
    def _():
        sout_ref[0] = S_ref[...]


def gdn_scan(qkv, gb, gbt, s0):
    B, T, D3 = qkv.shape
    D = D3 // 3
    C = GDN_CHUNK
    n = T // C
    W = gb.shape[2]
    R = gbt.shape[1]
    fwd = lambda j: (lambda b, i: (b, i, j))
    bwd = lambda j: (lambda b, i: (b, n - 1 - i, j))
    st = lambda b, i: (b, 0, 0, 0, 0)
    sspec = pl.BlockSpec((1, 2, GDN_HEADS, GDN_HEAD_DIM, GDN_HEAD_DIM), st)
    return pl.pallas_call(
        functools.partial(_gdn_scan_kernel, nsteps=n),
        grid=(B, n),
        in_specs=[pl.BlockSpec((1, C, D), fwd(0)), pl.BlockSpec((1, C, D), fwd(1)), pl.BlockSpec((1, C, D), fwd(2)),
                  pl.BlockSpec((1, C, W), fwd(0)), pl.BlockSpec((1, R, C), lambda b, i: (b, 0, i)),
                  pl.BlockSpec((1, C, D), bwd(0)), pl.BlockSpec((1, C, D), bwd(1)), pl.BlockSpec((1, C, D), bwd(2)),
                  pl.BlockSpec((1, C, W), bwd(0)), pl.BlockSpec((1, R, C), lambda b, i: (b, 0, n - 1 - i)),
                  sspec],
        out_specs=[pl.BlockSpec((1, C, D), fwd(0)), pl.BlockSpec((1, C, D), bwd(0)), sspec],
        out_shape=[jax.ShapeDtypeStruct((B, T, D), F32), jax.ShapeDtypeStruct((B, T, D), F32),
                   jax.ShapeDtypeStruct(s0.shape, F32)],
        scratch_shapes=[pltpu.VMEM((2, GDN_HEADS, GDN_HEAD_DIM, GDN_HEAD_DIM), F32)],
        compiler_params=_params(2, VMEM_MID),
        name="gdn_scan",
    )(qkv, qkv, qkv, gb, gbt, qkv, qkv, qkv, gb, gbt, s0)


def _router_kernel(x_ref, sc_ref, sh_ref, wr_ref, o_ref):
    h = x_ref[0] * (1.0 + sc_ref[0]) + sh_ref[0]
    lg = lax.dot_general(wr_ref[...], h, _NT, preferred_element_type=F32, precision=HIGHEST)
    e = jnp.exp(lg - jnp.max(lg, 0, keepdims=True))
    o_ref[0] = e / jnp.sum(e, 0, keepdims=True)


def router(x, sc, sh, w_router_t, tt):
    B, n, D = x.shape
    E = w_router_t.shape[0]
    tt = min(tt, n)
    return pl.pallas_call(
        _router_kernel,
        grid=(B, n // tt),
        in_specs=[pl.BlockSpec((1, tt, D), lambda b, i: (b, i, 0)),
                  pl.BlockSpec((1, 1, D), lambda b, i: (b, 0, 0)),
                  pl.BlockSpec((1, 1, D), lambda b, i: (b, 0, 0)),
                  pl.BlockSpec((E, D), lambda b, i: (0, 0))],
        out_specs=pl.BlockSpec((1, E, tt), lambda b, i: (b, 0, i)),
        out_shape=jax.ShapeDtypeStruct((B, E, n), F32),
        compiler_params=_params(2, VMEM_MID),
        name="router",
    )(x, sc, sh, w_router_t)


def _exclusive_cumsum_lanes(mask, n):
    r = lax.broadcasted_iota(I32, (LANES, LANES), 0)
    c = lax.broadcasted_iota(I32, (LANES, LANES), 1)
    upper = jnp.where(r < c, 1.0, 0.0).astype(BF16)
    carry = jnp.zeros((mask.shape[0], 1), F32)
    out = []
    for j in range(n // LANES):
        blk = mask[:, j * LANES:(j + 1) * LANES]
        out.append(jnp.dot(blk.astype(BF16), upper, preferred_element_type=F32) + carry)
        carry = carry + jnp.sum(blk, -1, keepdims=True)
    return jnp.concatenate(out, axis=1)


def _topk_kernel(aff_ref, idx_ref, gate_ref, pos_ref, lhs_ref, res_ref, *, n, cap, tb):
    E = aff_ref.shape[1]
    aff = aff_ref[0]
    bits = pltpu.bitcast(aff, I32)
    count = lambda m: jnp.sum(jnp.where(m, 1.0, 0.0), -1, keepdims=True)
    t = jnp.zeros((E, 1), I32)
    for bit in range(30, -1, -1):
        cand = t | (1 << bit)
        t = jnp.where(count(bits >= cand) >= cap, cand, t)
    gt = bits > t
    eq = bits == t
    need = cap - count(gt)
    eq_rank = _exclusive_cumsum_lanes(jnp.where(eq, 1.0, 0.0), n)
    sel = gt | (eq & (eq_rank < need))
    pos = _exclusive_cumsum_lanes(jnp.where(sel, 1.0, 0.0), n)
    pos_ref[...] = jnp.where(sel, pos, -1.0)
    tok = lax.broadcasted_iota(I32, (E, n), 1)
    row = lax.broadcasted_iota(I32, (E, n), 0)
    tokrows = jnp.where(row == 0, tok // 64, jnp.where(row == 1, tok % 64, 0)).astype(F32)
    g_hi = aff.astype(BF16)
    r1 = aff - g_hi.astype(F32)
    g_mid = r1.astype(BF16)
    g_lo = (r1 - g_mid.astype(F32)).astype(BF16)
    lhs_ref[0 * E:1 * E, :] = tokrows.astype(BF16)
    lhs_ref[1 * E:2 * E, :] = g_hi
    lhs_ref[2 * E:3 * E, :] = g_mid
    lhs_ref[3 * E:4 * E, :] = g_lo
    slot = lax.broadcasted_iota(I32, (cap, tb), 0).astype(F32)

    def per_expert(e, carry):
        acc = jnp.zeros((4 * E, cap), F32)
        for j in range(n // tb):
            blk = slice(j * tb, (j + 1) * tb)
            onehot = jnp.where(pos_ref[pl.ds(e, 1), blk] == slot, 1.0, 0.0).astype(BF16)
            acc = acc + lax.dot_general(lhs_ref[:, blk], onehot, _NT, preferred_element_type=F32)
        res_ref[...] = acc
        idx_row = res_ref[0:1, :] * 64.0 + res_ref[1:2, :]
        idx_ref[0, pl.ds(e, 1), :] = idx_row.astype(I32)
        gate_ref[0, pl.ds(e, 1), :] = (res_ref[pl.ds(E + e, 1), :] + res_ref[pl.ds(2 * E + e, 1), :]
                                       + res_ref[pl.ds(3 * E + e, 1), :])
        return carry

    lax.fori_loop(0, E, per_expert, 0)


def topk_select(aff, cap):
    B, E, n = aff.shape
    tb = min(512, n)
    return pl.pallas_call(
        functools.partial(_topk_kernel, n=n, cap=cap, tb=tb),
        grid=(B,),
        in_specs=[pl.BlockSpec((1, E, n), lambda b: (b, 0, 0))],
        out_specs=[pl.BlockSpec((1, E, cap), lambda b: (b, 0, 0)),
                   pl.BlockSpec((1, E, cap), lambda b: (b, 0, 0))],
        out_shape=[jax.ShapeDtypeStruct((B, E, cap), I32), jax.ShapeDtypeStruct((B, E, cap), F32)],
        scratch_shapes=[pltpu.VMEM((E, n), F32), pltpu.VMEM((4 * E, n), BF16), pltpu.VMEM((4 * E, cap), F32)],
        compiler_params=_params(1, VMEM_MID),
        name="topk_select",
    )(aff)


def _moe_kernel(idx_ref, gate_ref, x_ref, sc_ref, sh_ref, wg_ref, wu_ref, wd_ref, y_ref,
                xs32_ref, xs_ref, ye_ref, yp_ref, *, S, nfc):
    e = pl.program_id(0)
    f = pl.program_id(1)
    E = pl.num_programs(0)
    per_step = S // nfc // SUBLANES

    def gather8(expert, base):
        for j in range(SUBLANES):
            i = idx_ref[expert * S + base + j]
            xs32_ref[pl.ds(base + j, 1), :] = x_ref[0, pl.ds(i, 1), :]

    def scatter8(expert, base, scale):
        rows, news = [], []
        for j in range(SUBLANES):
            i = idx_ref[expert * S + base + j]
            g = gate_ref[expert * S + base + j] * scale
            rows.append(pl.ds(i, 1))
            news.append(y_ref[0, pl.ds(i, 1), :] + g * yp_ref[pl.ds(base + j, 1), :])
        for r, new in zip(rows, news):
            y_ref[0, r, :] = new

    def modulated_bf16():
        return (xs32_ref[...] * (1.0 + sc_ref[0]) + sh_ref[0]).astype(BF16)

    @pl.when((e == 0) & (f == 0))
    def _():
        y_ref[0] = jnp.zeros(y_ref.shape[1:], F32)
        yp_ref[...] = jnp.zeros(yp_ref.shape, F32)

        def body(s8, carry):
            gather8(0, pl.multiple_of(s8 * SUBLANES, SUBLANES))
            return carry

        lax.fori_loop(0, S // SUBLANES, body, 0)
        xs_ref[...] = modulated_bf16()

    prev = jnp.maximum(e - 1, 0)
    prev_scale = jnp.where(e > 0, 1.0, 0.0)
    nxt = jnp.minimum(e + 1, E - 1)
    for s8 in range(per_step):
        base = pl.multiple_of((f * per_step + s8) * SUBLANES, SUBLANES)
        scatter8(prev, base, prev_scale)
        gather8(nxt, base)

    xs = xs_ref[...]
    a = jnp.dot(xs, wg_ref[0], preferred_element_type=F32)
    u = jnp.dot(xs, wu_ref[0], preferred_element_type=F32)
    part = jnp.dot((_silu(a) * u).astype(BF16), wd_ref[0], preferred_element_type=F32)

    @pl.when(f == 0)
    def _():
        ye_ref[...] = part

    @pl.when(f > 0)
    def _():
        ye_ref[...] += part

    @pl.when(f == nfc - 1)
    def _():
        yp_ref[...] = ye_ref[...]
        xs_ref[...] = modulated_bf16()

    @pl.when((e == E - 1) & (f == nfc - 1))
    def _():
        def body(s8, carry):
            scatter8(e, pl.multiple_of(s8 * SUBLANES, SUBLANES), 1.0)
            return carry

        lax.fori_loop(0, S // SUBLANES, body, 0)


def moe_experts(idx, gate, x, g, sc, sh, wg, wu, wd, fc):
    G, R, D = x.shape
    E, _, F = wg.shape
    S = idx.shape[0] // E
    nfc = F // fc
    return pl.pallas_call(
        functools.partial(_moe_kernel, S=S, nfc=nfc),
        grid_spec=pltpu.PrefetchScalarGridSpec(
            num_scalar_prefetch=2,
            grid=(E, nfc),
            in_specs=[pl.BlockSpec((1, R, D), lambda e, f, *_: (g, 0, 0), pipeline_mode=pl.Buffered(1)),
                      pl.BlockSpec((1, 1, D), lambda e, f, *_: (g, 0, 0)),
                      pl.BlockSpec((1, 1, D), lambda e, f, *_: (g, 0, 0)),
                      pl.BlockSpec((1, D, fc), lambda e, f, *_: (e, 0, f)),
                      pl.BlockSpec((1, D, fc), lambda e, f, *_: (e, 0, f)),
                      pl.BlockSpec((1, fc, D), lambda e, f, *_: (e, f, 0))],
            out_specs=pl.BlockSpec((1, R, D), lambda e, f, *_: (0, 0, 0), pipeline_mode=pl.Buffered(1)),
            scratch_shapes=[pltpu.VMEM((S, D), F32), pltpu.VMEM((S, D), BF16), pltpu.VMEM((S, D), F32),
                            pltpu.VMEM((S, D), F32)]),
        out_shape=jax.ShapeDtypeStruct((1, R, D), F32),
        compiler_params=_params(2, VMEM_BIG),
        name="moe_experts",
    )(idx, gate, x, sc, sh, wg, wu, wd)[0]


def ec_moe(x, sc, sh, w_router_t, wg, wu, wd, flatten_groups):
    B, n, D = x.shape
    E = w_router_t.shape[0]
    cap = EC_CAPACITY_FACTOR * n // E
    aff = router(x, sc, sh, w_router_t, 512)
    idx, gate = topk_select(aff, cap)
    fc = min(MOE_HIDDEN_CHUNK, wg.shape[2])
    if flatten_groups:
        idx = idx + (jnp.arange(B, dtype=I32) * n)[:, None, None]
        idx = idx.transpose(1, 0, 2).reshape(-1)
        gate = gate.transpose(1, 0, 2).reshape(-1)
        y = moe_experts(idx, gate, x.reshape(1, B * n, D), 0, sc[:1], sh[:1], wg, wu, wd, fc)
        return y.reshape(B, n, D)
    ys = [moe_experts(idx[b].reshape(-1), gate[b].reshape(-1), x, b, sc, sh, wg, wu, wd, fc) for b in range(B)]
    return jnp.stack(ys)


def kernel(x, c, ctx, c_ctx, ada_w, ada_b, ln_g, ln_b, na_w_qkv, na_w_o, na_rpb, gdn_w_in, gdn_conv_w,
           gdn_a_log, gdn_dt_bias, gdn_norm_w, gdn_w_o, moe_w_router, moe_w_gate, moe_w_up, moe_w_down):
    B, N, D = x.shape
    L = ctx.shape[1]
    depth = ada_w.shape[0]
    alpha = (2.0 * depth) ** 0.25
    xc = ctx

    rows = -(-(B + 1) // SUBLANES) * SUBLANES
    cs = jnp.zeros((rows, D), F32).at[:B].set(c).at[B].set(c_ctx)
    mods = ada_modulation(cs, ada_w, ada_b)

    for l in range(depth):
        last = l == depth - 1
        i = l // 2
        mod = [mods[l, :B, j * D:(j + 1) * D].reshape(B, 1, D) for j in range(6)]
        modc = [jnp.broadcast_to(mods[l, B, j * D:(j + 1) * D].reshape(1, 1, D), (B, 1, D)) for j in range(6)]
        g1, b1, g2, b2 = ln_g[l, 0], ln_b[l, 0], ln_g[l, 1], ln_b[l, 1]
        if l % 2 == 0:
            qkv = mm_mod(x, mod[1], mod[0], na_w_qkv[i], BF16, 1024, 1024)
            qkvc = mm_mod(xc, modc[1], modc[0], na_w_qkv[i], BF16, 1024, 1024)
            o = natten(qkv, qkvc, _natten_bias_table(na_rpb[i]))
            x = mm_ln(o, x, mod[2], na_w_o[i], g1, b1, alpha, 512)
            if not last:
                xc = mm_ln(ctx_attention(qkvc), xc, modc[2], na_w_o[i], g1, b1, alpha, 512)
        else:
            w_main = gdn_w_in[i][:, :4 * D]
            w_ab = jnp.pad(gdn_w_in[i][:, 4 * D:], ((0, 0), (0, LANES - 4 * GDN_HEADS)))

            def project(u, m):
                p = mm_mod(u, m[1], m[0], w_main, F32, 1024, 1024)
                ab = mm_mod(u, m[1], m[0], w_ab, F32, 1024, LANES)
                gb = gdn_gates(ab, gdn_a_log[i], gdn_dt_bias[i], 1024)
                gbt = jnp.swapaxes(gb[:, :, :4 * GDN_HEADS], 1, 2)
                return p, gdn_conv(p, gdn_conv_w[i], 512), gb, gbt

            p, qkv, gb, gbt = project(x, mod)
            pc, qkvc, gbc, gbtc = project(xc, modc)
            s0 = jnp.zeros((B, 2, GDN_HEADS, GDN_HEAD_DIM, GDN_HEAD_DIM), F32)
            oc_f, oc_b, s_ctx = gdn_scan(qkvc, gbc, gbtc, s0)
            o_f, o_b, _ = gdn_scan(qkv, gb, gbt, s_ctx)
            x = mm_ln_gdn(o_f, o_b, p, gdn_norm_w[i], x, mod[2], gdn_w_o[i], g1, b1, alpha, 512)
            if not last:
                xc = mm_ln_gdn(oc_f, oc_b, pc, gdn_norm_w[i], xc, modc[2], gdn_w_o[i], g1, b1, alpha, 512)
        wr_t = moe_w_router[l].T
        wg, wu, wd = moe_w_gate[l].astype(BF16), moe_w_up[l].astype(BF16), moe_w_down[l].astype(BF16)
        y = ec_moe(x, mod[4], mod[3], wr_t, wg, wu, wd, False)
        x = res_ln(y, x, mod[5], g2, b2, alpha, 512)
        if not last:
            yc = ec_moe(xc, modc[4], modc[3], wr_t, wg, wu, wd, True)
            xc = res_ln(yc, xc, modc[5], g2, b2, alpha, 512)
    return x
```

```python
import functools
import math

import numpy as np
import jax
import jax.numpy as jnp
from jax import lax
from jax.experimental import pallas as pl
from jax.experimental.pallas import tpu as pltpu

F32 = jnp.float32
BF16 = jnp.bfloat16
I32 = jnp.int32
HIGHEST = lax.Precision.HIGHEST

GRID_W = 64
NA_HEADS = 16
NA_HEAD_DIM = 64
NA_KR = 8
NA_KC = 16
GDN_HEADS = 8
GDN_HEAD_DIM = 128
GDN_CONV = 5
N_EXPERTS = 16
EC_CAPACITY_FACTOR = 2
LN_EPS = 1e-6
NEG_INF = -1e30

GDN_CHUNK = 128
CONV_HALO = 8
SUBLANES = 8
LANES = 128
MOE_HIDDEN_CHUNK = 1024
MOE_LN_ROWS = 256
TOPK_LO = 16
V7X_VMEM_BYTES = 64 * 1024 * 1024
VMEM_BIG = 56 * 1024 * 1024
VMEM_MID = 40 * 1024 * 1024

_NT = (((1,), (1,)), ((), ()))
_TN = (((0,), (0,)), ((), ()))


def _params(n_axes, vmem):
    return pltpu.CompilerParams(dimension_semantics=("arbitrary",) * n_axes, vmem_limit_bytes=vmem)


def _silu(x):
    return x * jax.nn.sigmoid(x)


def _layernorm_rows(z, g, b):
    mu = jnp.mean(z, -1, keepdims=True)
    zc = z - mu
    var = jnp.mean(zc * zc, -1, keepdims=True)
    return zc * lax.rsqrt(var + LN_EPS) * g + b


def _ada_kernel(cs_ref, w_ref, b_ref, o_ref):
    s = _silu(cs_ref[...])
    o_ref[0] = jnp.dot(s, w_ref[0], preferred_element_type=F32, precision=HIGHEST) + b_ref[0]


def ada_modulation(cs, ada_w, ada_b):
    depth, D, D6 = ada_w.shape
    R = cs.shape[0]
    tn = D6 // 4
    return pl.pallas_call(
        _ada_kernel,
        grid=(depth, D6 // tn),
        in_specs=[pl.BlockSpec((R, D), lambda l, j: (0, 0)),
                  pl.BlockSpec((1, D, tn), lambda l, j: (l, 0, j)),
                  pl.BlockSpec((1, 1, tn), lambda l, j: (l, 0, j))],
        out_specs=pl.BlockSpec((1, R, tn), lambda l, j: (l, 0, j)),
        out_shape=jax.ShapeDtypeStruct((depth, R, D6), F32),
        compiler_params=_params(2, VMEM_MID),
        name="ada_modulation",
    )(cs, ada_w, ada_b.reshape(depth, 1, D6))


def _mm_mod_kernel(x_ref, sc_ref, sh_ref, w_ref, o_ref):
    h = x_ref[0] * (1.0 + sc_ref[0]) + sh_ref[0]
    o_ref[0] = jnp.dot(h.astype(BF16), w_ref[...], preferred_element_type=F32).astype(o_ref.dtype)


def mm_mod(x, sc, sh, w, out_dtype, tm):
    B, N, D = x.shape
    NO = w.shape[1]
    tm = min(tm, N)
    return pl.pallas_call(
        _mm_mod_kernel,
        grid=(B, N // tm),
        in_specs=[pl.BlockSpec((1, tm, D), lambda b, i: (b, i, 0)),
                  pl.BlockSpec((1, 1, D), lambda b, i: (b, 0, 0)),
                  pl.BlockSpec((1, 1, D), lambda b, i: (b, 0, 0)),
                  pl.BlockSpec((D, NO), lambda b, i: (0, 0), pipeline_mode=pl.Buffered(1))],
        out_specs=pl.BlockSpec((1, tm, NO), lambda b, i: (b, i, 0)),
        out_shape=jax.ShapeDtypeStruct((B, N, NO), out_dtype),
        compiler_params=_params(2, VMEM_MID),
        name="mm_mod",
    )(x, sc, sh, w)


def _natten_bias_table(rpb):
    H = rpb.shape[0]
    qc = np.arange(GRID_W)
    kc = np.arange(GRID_W)
    col_start = np.clip(qc - NA_KC // 2, 0, GRID_W - NA_KC)
    valid = (kc[None, :] >= col_start[:, None]) & (kc[None, :] < col_start[:, None] + NA_KC)
    dc = np.clip(kc[None, :] - qc[:, None], -(NA_KC - 1), NA_KC - 1) + NA_KC - 1
    tab = rpb.astype(F32)[:, :, dc] + jnp.where(jnp.asarray(valid), 0.0, NEG_INF).astype(F32)[None, None]
    variants = [tab[:, d0:d0 + NA_KR].transpose(0, 2, 1, 3).reshape(H, GRID_W, NA_KR * GRID_W)
                for d0 in range(NA_KR)]
    return jnp.stack(variants)


def _softmax_pv(s_list, v_list):
    m = functools.reduce(jnp.maximum, [jnp.max(s, -1, keepdims=True) for s in s_list])
    ps = [jnp.exp(s - m) for s in s_list]
    den = functools.reduce(lambda a, b: a + b, [jnp.sum(p, -1, keepdims=True) for p in ps])
    o = functools.reduce(lambda a, b: a + b,
                         [jnp.dot(p.astype(BF16), v, preferred_element_type=F32) for p, v in zip(ps, v_list)])
    return o / den


def _natten_kernel(q_ref, k_ref, v_ref, kc_ref, vc_ref, bias_ref, o_ref, *, rows, scale):
    r = pl.program_id(1)
    rs = jnp.clip(r - NA_KR // 2, 0, rows - NA_KR)
    start = pl.multiple_of(rs * GRID_W, GRID_W)
    win = pl.ds(start, NA_KR * GRID_W)
    lane = lax.broadcasted_iota(I32, (GRID_W, LANES), 1)
    per_tile = LANES // NA_HEAD_DIM
    masks = [(lane >= half * NA_HEAD_DIM) & (lane < (half + 1) * NA_HEAD_DIM) for half in range(per_tile)]
    tiles = [slice(pair * LANES, (pair + 1) * LANES) for pair in range(NA_HEADS // per_tile)]
    scores = []
    for pair, sl in enumerate(tiles):
        q2 = q_ref[0, :, sl] * scale
        for half in range(per_tile):
            qh = jnp.where(masks[half], q2, jnp.zeros_like(q2))
            s_w = lax.dot_general(qh, k_ref[0, win, sl], _NT, preferred_element_type=F32)
            s_c = lax.dot_general(qh, kc_ref[0, :, sl], _NT, preferred_element_type=F32)
            scores.append((s_w + bias_ref[0, pair * per_tile + half], s_c))
    maxes = [jnp.maximum(jnp.max(s_w, -1, keepdims=True), jnp.max(s_c, -1, keepdims=True)) for s_w, s_c in scores]
    probs = [(jnp.exp(s_w - m), jnp.exp(s_c - m)) for (s_w, s_c), m in zip(scores, maxes)]
    dens = [jnp.sum(p_w, -1, keepdims=True) + jnp.sum(p_c, -1, keepdims=True) for p_w, p_c in probs]
    for pair, sl in enumerate(tiles):
        o = None
        for half in range(per_tile):
            h = pair * per_tile + half
            p_w, p_c = probs[h]
            pv = (jnp.dot(p_w.astype(BF16), v_ref[0, win, sl], preferred_element_type=F32)
                  + jnp.dot(p_c.astype(BF16), vc_ref[0, :, sl], preferred_element_type=F32)) / dens[h]
            o = pv if o is None else jnp.where(masks[half], pv, o)
        o_ref[0, :, sl] = o.astype(o_ref.dtype)


def natten(qkv, qkvc, bias_tab):
    B, N, D3 = qkv.shape
    D = D3 // 3
    L = qkvc.shape[1]
    rows = N // GRID_W
    assert rows >= NA_KR and N % GRID_W == 0
    scale = NA_HEAD_DIM ** -0.5
    assert math.frexp(scale)[0] == 0.5, "q is pre-scaled in bf16: the scale must be a power of two"

    def bias_index(b, r):
        rs = jnp.clip(r - NA_KR // 2, 0, rows - NA_KR)
        return (rs - r + NA_KR - 1, 0, 0, 0)

    return pl.pallas_call(
        functools.partial(_natten_kernel, rows=rows, scale=scale),
        grid=(B, rows),
        in_specs=[pl.BlockSpec((1, GRID_W, D), lambda b, r: (b, r, 0)),
                  pl.BlockSpec((1, N, D), lambda b, r: (b, 0, 1)),
                  pl.BlockSpec((1, N, D), lambda b, r: (b, 0, 2)),
                  pl.BlockSpec((1, L, D), lambda b, r: (b, 0, 1)),
                  pl.BlockSpec((1, L, D), lambda b, r: (b, 0, 2)),
                  pl.BlockSpec((1, NA_HEADS, GRID_W, NA_KR * GRID_W), bias_index)],
        out_specs=pl.BlockSpec((1, GRID_W, D), lambda b, r: (b, r, 0)),
        out_shape=jax.ShapeDtypeStruct((B, N, D), BF16),
        compiler_params=_params(2, VMEM_BIG),
        name="natten",
    )(qkv, qkv, qkv, qkvc, qkvc, bias_tab)


def _ctx_attn_kernel(q_ref, k_ref, v_ref, o_ref, *, scale):
    for h in range(NA_HEADS):
        sl = slice(h * NA_HEAD_DIM, (h + 1) * NA_HEAD_DIM)
        s = lax.dot_general(q_ref[0, :, sl], k_ref[0, :, sl], _NT, preferred_element_type=F32) * scale
        o_ref[0, :, sl] = _softmax_pv([s], [v_ref[0, :, sl]]).astype(o_ref.dtype)


def ctx_attention(qkvc):
    B, L, D3 = qkvc.shape
    D = D3 // 3
    return pl.pallas_call(
        functools.partial(_ctx_attn_kernel, scale=NA_HEAD_DIM ** -0.5),
        grid=(B,),
        in_specs=[pl.BlockSpec((1, L, D), lambda b: (b, 0, 0)),
                  pl.BlockSpec((1, L, D), lambda b: (b, 0, 1)),
                  pl.BlockSpec((1, L, D), lambda b: (b, 0, 2))],
        out_specs=pl.BlockSpec((1, L, D), lambda b: (b, 0, 0)),
        out_shape=jax.ShapeDtypeStruct((B, L, D), BF16),
        compiler_params=_params(1, VMEM_MID),
        name="ctx_attention",
    )(qkvc, qkvc, qkvc)


def _is_first_step():
    return (pl.program_id(0) == 0) & (pl.program_id(1) == 0)


def _mm_ln_kernel(a_ref, x_ref, gate_ref, w_ref, g_ref, b_ref, o_ref, wb_ref, *, alpha):
    @pl.when(_is_first_step())
    def _():
        wb_ref[...] = w_ref[...].astype(BF16)

    y = jnp.dot(a_ref[0], wb_ref[...], preferred_element_type=F32)
    o_ref[0] = _layernorm_rows(alpha * x_ref[0] + gate_ref[0] * y, g_ref[...], b_ref[...])


def mm_ln(a, x, gate, w, g, b, alpha, tm):
    B, N, D = x.shape
    tm = min(tm, N)
    row = lambda bi, i: (bi, i, 0)
    vec = lambda bi, i: (0, 0)
    return pl.pallas_call(
        functools.partial(_mm_ln_kernel, alpha=alpha),
        grid=(B, N // tm),
        in_specs=[pl.BlockSpec((1, tm, D), row),
                  pl.BlockSpec((1, tm, D), row),
                  pl.BlockSpec((1, 1, D), lambda bi, i: (bi, 0, 0)),
                  pl.BlockSpec((D, D), vec),
                  pl.BlockSpec((1, D), vec),
                  pl.BlockSpec((1, D), vec)],
        out_specs=pl.BlockSpec((1, tm, D), row),
        out_shape=jax.ShapeDtypeStruct((B, N, D), F32),
        scratch_shapes=[pltpu.VMEM((D, D), BF16)],
        compiler_params=_params(2, VMEM_MID),
        name="mm_ln",
    )(a, x, gate, w, g.reshape(1, D), b.reshape(1, D))


def _mm_ln_gdn_kernel(of_ref, ob_ref, z_ref, nw_ref, x_ref, gate_ref, w_ref, g_ref, b_ref, o_ref,
                      wb_ref, a_ref, *, alpha):
    @pl.when(_is_first_step())
    def _():
        wb_ref[...] = w_ref[...].astype(BF16)

    for h in range(GDN_HEADS):
        sl = slice(h * GDN_HEAD_DIM, (h + 1) * GDN_HEAD_DIM)
        o = of_ref[0, :, sl] + ob_ref[0, :, sl]
        y = o * lax.rsqrt(jnp.mean(o * o, -1, keepdims=True) + LN_EPS) * nw_ref[...]
        a_ref[:, sl] = (y * _silu(z_ref[0, :, sl])).astype(BF16)
    y = jnp.dot(a_ref[...], wb_ref[...], preferred_element_type=F32)
    o_ref[0] = _layernorm_rows(alpha * x_ref[0] + gate_ref[0] * y, g_ref[...], b_ref[...])


def mm_ln_gdn(o_f, o_b, p, norm_w, x, gate, w, g, b, alpha, tm):
    B, N, D = x.shape
    tm = min(tm, N)
    row = lambda bi, i: (bi, i, 0)
    vec = lambda bi, i: (0, 0)
    return pl.pallas_call(
        functools.partial(_mm_ln_gdn_kernel, alpha=alpha),
        grid=(B, N // tm),
        in_specs=[pl.BlockSpec((1, tm, D), row),
                  pl.BlockSpec((1, tm, D), row),
                  pl.BlockSpec((1, tm, D), lambda bi, i: (bi, i, 3)),
                  pl.BlockSpec((1, GDN_HEAD_DIM), vec),
                  pl.BlockSpec((1, tm, D), row),
                  pl.BlockSpec((1, 1, D), lambda bi, i: (bi, 0, 0)),
                  pl.BlockSpec((D, D), vec),
                  pl.BlockSpec((1, D), vec),
                  pl.BlockSpec((1, D), vec)],
        out_specs=pl.BlockSpec((1, tm, D), row),
        out_shape=jax.ShapeDtypeStruct((B, N, D), F32),
        scratch_shapes=[pltpu.VMEM((D, D), BF16), pltpu.VMEM((tm, D), BF16)],
        compiler_params=_params(2, VMEM_MID),
        name="mm_ln_gdn",
    )(o_f, o_b, p, norm_w.reshape(1, GDN_HEAD_DIM), x, gate, w, g.reshape(1, D), b.reshape(1, D))


def _conv_kernel(prev_ref, main_ref, next_ref, w_ref, o_ref, ext_ref, *, tt, nt):
    i = pl.program_id(1)
    j = pl.program_id(2)
    ext_ref[0:CONV_HALO, :] = jnp.where(i > 0, prev_ref[0], 0.0)
    ext_ref[CONV_HALO:CONV_HALO + tt, :] = main_ref[0]
    ext_ref[CONV_HALO + tt:, :] = jnp.where(i < nt - 1, next_ref[0], 0.0)
    pad = GDN_CONV // 2
    acc = w_ref[0:1, :] * ext_ref[CONV_HALO - pad:CONV_HALO - pad + tt, :]
    for t in range(1, GDN_CONV):
        acc = acc + w_ref[t:t + 1, :] * ext_ref[CONV_HALO - pad + t:CONV_HALO - pad + t + tt, :]
    y = _silu(acc)
    qscale = jnp.where(j == 0, GDN_HEAD_DIM ** -0.5, 1.0)
    for h in range(GDN_HEADS):
        sl = slice(h * GDN_HEAD_DIM, (h + 1) * GDN_HEAD_DIM)
        seg = y[:, sl]
        inv = lax.rsqrt(jnp.sum(seg * seg, -1, keepdims=True) + 1e-6) * qscale
        o_ref[0, :, sl] = seg * jnp.where(j < 2, inv, 1.0)


def gdn_conv(p, conv_w, tt):
    B, T = p.shape[:2]
    D = conv_w.shape[1] // 3
    tt = min(tt, T)
    nt = T // tt
    hb = tt // CONV_HALO
    return pl.pallas_call(
        functools.partial(_conv_kernel, tt=tt, nt=nt),
        grid=(B, nt, 3),
        in_specs=[pl.BlockSpec((1, CONV_HALO, D), lambda b, i, j: (b, jnp.maximum(i * hb - 1, 0), j)),
                  pl.BlockSpec((1, tt, D), lambda b, i, j: (b, i, j)),
                  pl.BlockSpec((1, CONV_HALO, D), lambda b, i, j: (b, jnp.minimum((i + 1) * hb, T // CONV_HALO - 1), j)),
                  pl.BlockSpec((GDN_CONV, D), lambda b, i, j: (0, j))],
        out_specs=pl.BlockSpec((1, tt, D), lambda b, i, j: (b, i, j)),
        out_shape=jax.ShapeDtypeStruct((B, T, 3 * D), F32),
        scratch_shapes=[pltpu.VMEM((tt + 2 * CONV_HALO, D), F32)],
        compiler_params=_params(3, VMEM_MID),
        name="gdn_conv",
    )(p, p, p, conv_w)


def _gates_kernel(ab_ref, alog_ref, dtb_ref, o_ref):
    ab = ab_ref[0]
    xg = ab + dtb_ref[...]
    softplus = jnp.maximum(xg, 0.0) + jnp.log(1.0 + jnp.exp(-jnp.abs(xg)))
    g = -jnp.exp(alog_ref[...]) * softplus
    beta = jax.nn.sigmoid(ab)
    lane = lax.broadcasted_iota(I32, ab.shape, 1)
    o_ref[0] = jnp.where(lane < 2 * GDN_HEADS, g, beta)


def gdn_gates(ab, a_log, dt_bias, tt):
    B, T, W = ab.shape
    tt = min(tt, T)
    pad = lambda a: jnp.pad(a.reshape(1, -1).astype(F32), ((0, 0), (0, W - a.size)))
    return pl.pallas_call(
        _gates_kernel,
        grid=(B, T // tt),
        in_specs=[pl.BlockSpec((1, tt, W), lambda b, i: (b, i, 0)),
                  pl.BlockSpec((1, W), lambda b, i: (0, 0)),
                  pl.BlockSpec((1, W), lambda b, i: (0, 0))],
        out_specs=pl.BlockSpec((1, tt, W), lambda b, i: (b, i, 0)),
        out_shape=jax.ShapeDtypeStruct((B, T, W), F32),
        compiler_params=_params(2, VMEM_MID),
        name="gdn_gates",
    )(ab, pad(a_log), pad(dt_bias))


TRI_BASE = 16


def _bmm(a, b):
    return jnp.einsum("hij,hjk->hik", a.astype(BF16), b.astype(BF16), preferred_element_type=F32)


def _unit_triangular_inverse(A, eye):
    C = A.shape[-1]
    r = lax.broadcasted_iota(I32, (C, C), 0)
    c = lax.broadcasted_iota(I32, (C, C), 1)
    sh = int(math.log2(TRI_BASE))
    B = jnp.where((r >> sh) == (c >> sh), A, 0.0)
    T = eye - B
    P = B
    for _ in range(sh - 1):
        P = _bmm(P, P)
        T = T + _bmm(T, P)
    while sh < int(math.log2(C)):
        off = ((r >> (sh + 1)) == (c >> (sh + 1))) & ((r >> sh) != (c >> sh))
        T = T - _bmm(_bmm(T, jnp.where(off, A, 0.0)), T)
        sh += 1
    return T


def _gdn_chunk(d, q_ref, k_ref, v_ref, gb, Gc_all, Gr_all, S_ref, o_ref, incl, strict, eye):
    C, H, dk = GDN_CHUNK, GDN_HEADS, GDN_HEAD_DIM
    last = C - 1 if d == 0 else 0
    cols = [d * H + h for h in range(H)]
    heads = lambda f: jnp.stack([f(h, cols[h]) for h in range(H)])
    Gc = heads(lambda h, col: Gc_all[:, col:col + 1])
    Gr = heads(lambda h, col: Gr_all[col:col + 1, :])
    Gtot = heads(lambda h, col: Gc_all[last:last + 1, col:col + 1])
    beta = heads(lambda h, col: gb[:, 2 * H + col:2 * H + col + 1])
    q = heads(lambda h, col: q_ref[0, :, h * dk:(h + 1) * dk])
    k = heads(lambda h, col: k_ref[0, :, h * dk:(h + 1) * dk])
    v = heads(lambda h, col: v_ref[0, :, h * dk:(h + 1) * dk])
    decay = jnp.where(incl, jnp.exp(jnp.minimum(Gc - Gr, 0.0)), 0.0)
    kb = k * beta
    kq = jnp.einsum("hid,hjd->hij", jnp.concatenate([kb, q], axis=1).astype(BF16), k.astype(BF16),
                    preferred_element_type=F32)
    A = jnp.where(strict, kq[:, :C] * decay, 0.0)
    attn = kq[:, C:] * decay
    T = _unit_triangular_inverse(A, eye)
    eG = jnp.exp(Gc)
    wu = _bmm(T, jnp.concatenate([kb * eG, v * beta], axis=2))
    S = S_ref[d]
    wq = _bmm(jnp.concatenate([wu[:, :, :dk], q * eG], axis=1), S)
    v_new = wu[:, :, dk:] - wq[:, :C]
    o = wq[:, C:] + _bmm(attn, v_new)
    k_tail = k * jnp.exp(Gtot - Gc)
    S_ref[d] = S * jnp.exp(Gtot) + jnp.einsum("hcd,hce->hde", k_tail.astype(BF16), v_new.astype(BF16),
                                              preferred_element_type=F32)
    for h in range(H):
        o_ref[0, :, h * dk:(h + 1) * dk] = o[h]


def _gdn_scan_kernel(qf_ref, kf_ref, vf_ref, gbf_ref, gtf_ref, qb_ref, kb_ref, vb_ref, gbb_ref, gtb_ref,
                     s0_ref, of_ref, ob_ref, sout_ref, S_ref, *, nsteps):
    i = pl.program_id(1)

    @pl.when(i == 0)
    def _():
        S_ref[...] = s0_ref[0]

    C = GDN_CHUNK
    r = lax.broadcasted_iota(I32, (C, C), 0)
    c = lax.broadcasted_iota(I32, (C, C), 1)
    eye = jnp.where(r == c, 1.0, 0.0).astype(F32)
    streams = ((qf_ref, kf_ref, vf_ref, gbf_ref, gtf_ref, of_ref, r >= c, r > c),
               (qb_ref, kb_ref, vb_ref, gbb_ref, gtb_ref, ob_ref, r <= c, r < c))
    for d, (q_ref, k_ref, v_ref, gb_ref, gt_ref, o_ref, incl, strict) in enumerate(streams):
        tri = jnp.where(incl, 1.0, 0.0).astype(F32)
        gb = gb_ref[0]
        Gc_all = jnp.dot(tri, gb, preferred_element_type=F32, precision=HIGHEST)
        Gr_all = lax.dot_general(gt_ref[0], tri, _NT, preferred_element_type=F32, precision=HIGHEST)
        _gdn_chunk(d, q_ref, k_ref, v_ref, gb, Gc_all, Gr_all, S_ref, o_ref, incl, strict, eye)

    @pl.when(i == nsteps - 1)
    def _():
        sout_ref[0] = S_ref[...]


def gdn_scan(qkv, gb, gbt, s0):
    B, T, D3 = qkv.shape
    D = D3 // 3
    C = GDN_CHUNK
    n = T // C
    W = gb.shape[2]
    R = gbt.shape[1]
    fwd = lambda j: (lambda b, i: (b, i, j))
    bwd = lambda j: (lambda b, i: (b, n - 1 - i, j))
    st = lambda b, i: (b, 0, 0, 0, 0)
    sspec = pl.BlockSpec((1, 2, GDN_HEADS, GDN_HEAD_DIM, GDN_HEAD_DIM), st)
    return pl.pallas_call(
        functools.partial(_gdn_scan_kernel, nsteps=n),
        grid=(B, n),
        in_specs=[pl.BlockSpec((1, C, D), fwd(0)), pl.BlockSpec((1, C, D), fwd(1)), pl.BlockSpec((1, C, D), fwd(2)),
                  pl.BlockSpec((1, C, W), fwd(0)), pl.BlockSpec((1, R, C), lambda b, i: (b, 0, i)),
                  pl.BlockSpec((1, C, D), bwd(0)), pl.BlockSpec((1, C, D), bwd(1)), pl.BlockSpec((1, C, D), bwd(2)),
                  pl.BlockSpec((1, C, W), bwd(0)), pl.BlockSpec((1, R, C), lambda b, i: (b, 0, n - 1 - i)),
                  sspec],
        out_specs=[pl.BlockSpec((1, C, D), fwd(0)), pl.BlockSpec((1, C, D), bwd(0)), sspec],
        out_shape=[jax.ShapeDtypeStruct((B, T, D), F32), jax.ShapeDtypeStruct((B, T, D), F32),
                   jax.ShapeDtypeStruct(s0.shape, F32)],
        scratch_shapes=[pltpu.VMEM((2, GDN_HEADS, GDN_HEAD_DIM, GDN_HEAD_DIM), F32)],
        compiler_params=_params(2, VMEM_MID),
        name="gdn_scan",
    )(qkv, qkv, qkv, gb, gbt, qkv, qkv, qkv, gb, gbt, s0)


def _router_kernel(x_ref, sc_ref, sh_ref, wr_ref, o_ref):
    h = x_ref[0] * (1.0 + sc_ref[0]) + sh_ref[0]
    lg = lax.dot_general(wr_ref[...], h, _NT, preferred_element_type=F32, precision=HIGHEST)
    e = jnp.exp(lg - jnp.max(lg, 0, keepdims=True))
    o_ref[0] = e / jnp.sum(e, 0, keepdims=True)


def router(x, sc, sh, w_router_t, tt):
    B, n, D = x.shape
    E = w_router_t.shape[0]
    tt = min(tt, n)
    return pl.pallas_call(
        _router_kernel,
        grid=(B, n // tt),
        in_specs=[pl.BlockSpec((1, tt, D), lambda b, i: (b, i, 0)),
                  pl.BlockSpec((1, 1, D), lambda b, i: (b, 0, 0)),
                  pl.BlockSpec((1, 1, D), lambda b, i: (b, 0, 0)),
                  pl.BlockSpec((E, D), lambda b, i: (0, 0))],
        out_specs=pl.BlockSpec((1, E, tt), lambda b, i: (b, 0, i)),
        out_shape=jax.ShapeDtypeStruct((B, E, n), F32),
        compiler_params=_params(2, VMEM_MID),
        name="router",
    )(x, sc, sh, w_router_t)


def _exclusive_cumsum_lanes(mask, n):
    r = lax.broadcasted_iota(I32, (LANES, LANES), 0)
    c = lax.broadcasted_iota(I32, (LANES, LANES), 1)
    upper = jnp.where(r < c, 1.0, 0.0).astype(BF16)
    carry = jnp.zeros((mask.shape[0], 1), F32)
    out = []
    for j in range(n // LANES):
        blk = mask[:, j * LANES:(j + 1) * LANES]
        out.append(jnp.dot(blk.astype(BF16), upper, preferred_element_type=F32) + carry)
        carry = carry + jnp.sum(blk, -1, keepdims=True)
    return jnp.concatenate(out, axis=1)


def _topk_kernel(aff_ref, idx_ref, gate_ref, hi_ref, lo_ref, pieces_ref, *, n, cap, n_hi, n_hi_pad):
    E = aff_ref.shape[1]
    aff = aff_ref[0]
    bits = pltpu.bitcast(aff, I32)
    count = lambda m: jnp.sum(jnp.where(m, 1.0, 0.0), -1, keepdims=True)
    t = jnp.zeros((E, 1), I32)
    for bit in range(30, -1, -1):
        cand = t | (1 << bit)
        t = jnp.where(count(bits >= cand) >= cap, cand, t)
    gt = bits > t
    eq = bits == t
    need = cap - count(gt)
    eq_rank = _exclusive_cumsum_lanes(jnp.where(eq, 1.0, 0.0), n)
    sel = gt | (eq & (eq_rank < need))
    pos = _exclusive_cumsum_lanes(jnp.where(sel, 1.0, 0.0), n).astype(I32)
    hi_ref[...] = jnp.where(sel, pos >> int(math.log2(TOPK_LO)), -1)
    lo_ref[...] = pos & (TOPK_LO - 1)
    g_hi = aff.astype(BF16).astype(F32)
    g_mid = (aff - g_hi).astype(BF16).astype(F32)
    pieces_ref[0 * E:1 * E, :] = g_hi
    pieces_ref[1 * E:2 * E, :] = g_mid
    pieces_ref[2 * E:3 * E, :] = aff - g_hi - g_mid
    tok = lax.broadcasted_iota(I32, (1, n), 1)
    tok_hi = (tok >> 6).astype(F32)
    tok_lo = (tok & 63).astype(F32)
    hi_iota = lax.broadcasted_iota(I32, (n_hi_pad, n), 0)
    lo_iota = lax.broadcasted_iota(I32, (TOPK_LO, n), 0)

    def per_expert(e, carry):
        a = jnp.where(hi_ref[pl.ds(e, 1), :] == hi_iota, 1.0, 0.0).astype(BF16)
        b = jnp.where(lo_ref[pl.ds(e, 1), :] == lo_iota, 1.0, 0.0)
        vals = [tok_hi, tok_lo] + [pieces_ref[pl.ds(k * E + e, 1), :] for k in range(3)]
        rhs = jnp.concatenate([(b * v).astype(BF16) for v in vals], axis=0)
        res = lax.dot_general(a, rhs, _NT, preferred_element_type=F32)
        part = lambda k: res[:n_hi, k * TOPK_LO:(k + 1) * TOPK_LO]
        idx_ref[0, pl.ds(e, 1)] = (part(0) * 64.0 + part(1)).astype(I32)[None]
        gate_ref[0, pl.ds(e, 1)] = (part(2) + part(3) + part(4))[None]
        return carry

    lax.fori_loop(0, E, per_expert, 0)


def topk_select(aff, cap):
    B, E, n = aff.shape
    n_hi = cap // TOPK_LO
    n_hi_pad = -(-n_hi // SUBLANES) * SUBLANES
    idx, gate = pl.pallas_call(
        functools.partial(_topk_kernel, n=n, cap=cap, n_hi=n_hi, n_hi_pad=n_hi_pad),
        grid=(B,),
        in_specs=[pl.BlockSpec((1, E, n), lambda b: (b, 0, 0))],
        out_specs=[pl.BlockSpec((1, E, n_hi, TOPK_LO), lambda b: (b, 0, 0, 0)),
                   pl.BlockSpec((1, E, n_hi, TOPK_LO), lambda b: (b, 0, 0, 0))],
        out_shape=[jax.ShapeDtypeStruct((B, E, n_hi, TOPK_LO), I32), jax.ShapeDtypeStruct((B, E, n_hi, TOPK_LO), F32)],
        scratch_shapes=[pltpu.VMEM((E, n), I32), pltpu.VMEM((E, n), I32), pltpu.VMEM((3 * E, n), F32)],
        compiler_params=_params(1, VMEM_MID),
        name="topk_select",
    )(aff)
    return idx.reshape(B, E, cap), gate.reshape(B, E, cap)


def _moe_kernel(idx_ref, gate_ref, x_ref, sc_ref, sh_ref, og_ref, lng_ref, lnb_ref, wg_ref, wu_ref, wd_ref, y_ref,
                xs32_ref, xs_ref, ye_ref, yp_ref, *, S, nfc, alpha):
    e = pl.program_id(0)
    f = pl.program_id(1)
    E = pl.num_programs(0)
    per_step = S // nfc // SUBLANES

    def gather8(expert, base):
        for j in range(SUBLANES):
            i = idx_ref[expert * S + base + j]
            xs32_ref[pl.ds(base + j, 1), :] = x_ref[0, pl.ds(i, 1), :]

    def scatter8(expert, base, scale):
        rows, news = [], []
        for j in range(SUBLANES):
            i = idx_ref[expert * S + base + j]
            g = gate_ref[expert * S + base + j] * scale
            rows.append(pl.ds(i, 1))
            news.append(y_ref[0, pl.ds(i, 1), :] + g * yp_ref[pl.ds(base + j, 1), :])
        for r, new in zip(rows, news):
            y_ref[0, r, :] = new

    def modulated_bf16():
        return (xs32_ref[...] * (1.0 + sc_ref[0]) + sh_ref[0]).astype(BF16)

    @pl.when((e == 0) & (f == 0))
    def _():
        y_ref[0] = jnp.zeros(y_ref.shape[1:], F32)
        yp_ref[...] = jnp.zeros(yp_ref.shape, F32)

        def body(s8, carry):
            gather8(0, pl.multiple_of(s8 * SUBLANES, SUBLANES))
            return carry

        lax.fori_loop(0, S // SUBLANES, body, 0)
        xs_ref[...] = modulated_bf16()

    prev = jnp.maximum(e - 1, 0)
    prev_scale = jnp.where(e > 0, 1.0, 0.0)
    nxt = jnp.minimum(e + 1, E - 1)
    for s8 in range(per_step):
        base = pl.multiple_of((f * per_step + s8) * SUBLANES, SUBLANES)
        scatter8(prev, base, prev_scale)
        gather8(nxt, base)

    xs = xs_ref[...]
    a = jnp.dot(xs, wg_ref[0], preferred_element_type=F32)
    u = jnp.dot(xs, wu_ref[0], preferred_element_type=F32)
    part = jnp.dot((_silu(a) * u).astype(BF16), wd_ref[0], preferred_element_type=F32)

    @pl.when(f == 0)
    def _():
        ye_ref[...] = part

    @pl.when(f > 0)
    def _():
        ye_ref[...] += part

    @pl.when(f == nfc - 1)
    def _():
        yp_ref[...] = ye_ref[...]
        xs_ref[...] = modulated_bf16()

    @pl.when((e == E - 1) & (f == nfc - 1))
    def _():
        def body(s8, carry):
            scatter8(e, pl.multiple_of(s8 * SUBLANES, SUBLANES), 1.0)
            return carry

        lax.fori_loop(0, S // SUBLANES, body, 0)

        def ln_body(t, carry):
            rows = pl.ds(pl.multiple_of(t * MOE_LN_ROWS, MOE_LN_ROWS), MOE_LN_ROWS)
            z = alpha * x_ref[0, rows, :] + og_ref[0] * y_ref[0, rows, :]
            y_ref[0, rows, :] = _layernorm_rows(z, lng_ref[...], lnb_ref[...])
            return carry

        lax.fori_loop(0, y_ref.shape[1] // MOE_LN_ROWS, ln_body, 0)


def moe_experts(idx, gate, x, g, sc, sh, out_gate, ln_g, ln_b, wg, wu, wd, e0, n_experts, fc, alpha):
    G, R, D = x.shape
    F = wg.shape[2]
    E = n_experts
    S = idx.shape[0] // E
    nfc = F // fc
    grp = lambda e, f, *_: (g, 0, 0)
    vec = lambda e, f, *_: (0, 0)
    return pl.pallas_call(
        functools.partial(_moe_kernel, S=S, nfc=nfc, alpha=alpha),
        grid_spec=pltpu.PrefetchScalarGridSpec(
            num_scalar_prefetch=2,
            grid=(E, nfc),
            in_specs=[pl.BlockSpec((1, R, D), grp, pipeline_mode=pl.Buffered(1)),
                      pl.BlockSpec((1, 1, D), grp),
                      pl.BlockSpec((1, 1, D), grp),
                      pl.BlockSpec((1, 1, D), grp),
                      pl.BlockSpec((1, D), vec),
                      pl.BlockSpec((1, D), vec),
                      pl.BlockSpec((1, D, fc), lambda e, f, *_: (e0 + e, 0, f)),
                      pl.BlockSpec((1, D, fc), lambda e, f, *_: (e0 + e, 0, f)),
                      pl.BlockSpec((1, fc, D), lambda e, f, *_: (e0 + e, f, 0))],
            out_specs=pl.BlockSpec((1, R, D), grp, pipeline_mode=pl.Buffered(1)),
            scratch_shapes=[pltpu.VMEM((S, D), F32), pltpu.VMEM((S, D), BF16), pltpu.VMEM((S, D), F32),
                            pltpu.VMEM((S, D), F32)]),
        out_shape=jax.ShapeDtypeStruct((G, R, D), F32),
        input_output_aliases={2: 0},
        compiler_params=_params(2, VMEM_BIG),
        name="moe_experts",
    )(idx, gate, x, sc, sh, out_gate, ln_g.reshape(1, D), ln_b.reshape(1, D), wg, wu, wd)


def ec_moe_residual(x, sc, sh, out_gate, ln_g, ln_b, w_router_t, wg, wu, wd, e0, alpha, flatten_groups):
    B, n, D = x.shape
    E = w_router_t.shape[0]
    cap = EC_CAPACITY_FACTOR * n // E
    aff = router(x, sc, sh, w_router_t, 512)
    idx, gate = topk_select(aff, cap)
    fc = min(MOE_HIDDEN_CHUNK, wg.shape[2])
    if flatten_groups:
        idx = idx + (jnp.arange(B, dtype=I32) * n)[:, None, None]
        idx = idx.transpose(1, 0, 2).reshape(-1)
        gate = gate.transpose(1, 0, 2).reshape(-1)
        x = moe_experts(idx, gate, x.reshape(1, B * n, D), 0, sc[:1], sh[:1], out_gate[:1], ln_g, ln_b,
                        wg, wu, wd, e0, E, fc, alpha)
        return x.reshape(B, n, D)
    for b in range(B):
        x = moe_experts(idx[b].reshape(-1), gate[b].reshape(-1), x, b, sc, sh, out_gate, ln_g, ln_b,
                        wg, wu, wd, e0, E, fc, alpha)
    return x


def kernel(x, c, ctx, c_ctx, ada_w, ada_b, ln_g, ln_b, na_w_qkv, na_w_o, na_rpb, gdn_w_in, gdn_conv_w,
           gdn_a_log, gdn_dt_bias, gdn_norm_w, gdn_w_o, moe_w_router, moe_w_gate, moe_w_up, moe_w_down):
    B, N, D = x.shape
    L = ctx.shape[1]
    depth = ada_w.shape[0]
    alpha = (2.0 * depth) ** 0.25
    xc = ctx

    rows = -(-(B + 1) // SUBLANES) * SUBLANES
    cs = jnp.zeros((rows, D), F32).at[:B].set(c).at[B].set(c_ctx)
    mods = ada_modulation(cs, ada_w, ada_b)

    E = moe_w_gate.shape[1]
    stack = lambda w: w.astype(BF16).reshape((depth * E,) + w.shape[2:])
    wg_all, wu_all, wd_all = stack(moe_w_gate), stack(moe_w_up), stack(moe_w_down)

    for l in range(depth):
        last = l == depth - 1
        i = l // 2
        mod = [mods[l, :B, j * D:(j + 1) * D].reshape(B, 1, D) for j in range(6)]
        modc = [jnp.broadcast_to(mods[l, B, j * D:(j + 1) * D].reshape(1, 1, D), (B, 1, D)) for j in range(6)]
        g1, b1, g2, b2 = ln_g[l, 0], ln_b[l, 0], ln_g[l, 1], ln_b[l, 1]
        if l % 2 == 0:
            w_qkv = na_w_qkv[i].astype(BF16)
            qkv = mm_mod(x, mod[1], mod[0], w_qkv, BF16, 512)
            qkvc = mm_mod(xc, modc[1], modc[0], w_qkv, BF16, 512)
            o = natten(qkv, qkvc, _natten_bias_table(na_rpb[i]))
            x = mm_ln(o, x, mod[2], na_w_o[i], g1, b1, alpha, 512)
            if not last:
                xc = mm_ln(ctx_attention(qkvc), xc, modc[2], na_w_o[i], g1, b1, alpha, 512)
        else:
            w_main = gdn_w_in[i][:, :4 * D].astype(BF16)
            w_ab = jnp.pad(gdn_w_in[i][:, 4 * D:], ((0, 0), (0, LANES - 4 * GDN_HEADS))).astype(BF16)

            def project(u, m):
                p = mm_mod(u, m[1], m[0], w_main, F32, 512)
                ab = mm_mod(u, m[1], m[0], w_ab, F32, 1024)
                gb = gdn_gates(ab, gdn_a_log[i], gdn_dt_bias[i], 1024)
                gbt = jnp.swapaxes(gb[:, :, :4 * GDN_HEADS], 1, 2)
                return p, gdn_conv(p, gdn_conv_w[i], 512), gb, gbt

            p, qkv, gb, gbt = project(x, mod)
            pc, qkvc, gbc, gbtc = project(xc, modc)
            s0 = jnp.zeros((B, 2, GDN_HEADS, GDN_HEAD_DIM, GDN_HEAD_DIM), F32)
            oc_f, oc_b, s_ctx = gdn_scan(qkvc, gbc, gbtc, s0)
            o_f, o_b, _ = gdn_scan(qkv, gb, gbt, s_ctx)
            x = mm_ln_gdn(o_f, o_b, p, gdn_norm_w[i], x, mod[2], gdn_w_o[i], g1, b1, alpha, 512)
            if not last:
                xc = mm_ln_gdn(oc_f, oc_b, pc, gdn_norm_w[i], xc, modc[2], gdn_w_o[i], g1, b1, alpha, 512)
        wr_t = moe_w_router[l].T
        x = ec_moe_residual(x, mod[4], mod[3], mod[5], g2, b2, wr_t, wg_all, wu_all, wd_all, l * E, alpha, False)
        if not last:
            xc = ec_moe_residual(xc, modc[4], modc[3], modc[5], g2, b2, wr_t, wg_all, wu_all, wd_all, l * E, alpha,
                                 True)
    return x
```

```python
import functools
import math

import numpy as np
import jax
import jax.numpy as jnp
from jax import lax
from jax.experimental import pallas as pl
from jax.experimental.pallas import tpu as pltpu

F32 = jnp.float32
BF16 = jnp.bfloat16
I32 = jnp.int32
HIGHEST = lax.Precision.HIGHEST

GRID_W = 64
NA_HEADS = 16
NA_HEAD_DIM = 64
NA_KR = 8
NA_KC = 16
GDN_HEADS = 8
GDN_HEAD_DIM = 128
GDN_CONV = 5
N_EXPERTS = 16
EC_CAPACITY_FACTOR = 2
LN_EPS = 1e-6
NEG_INF = -1e30

GDN_CHUNK = 128
CONV_HALO = 8
SUBLANES = 8
LANES = 128
MOE_HIDDEN_CHUNK = 1024
MOE_LN_ROWS = 256
TOPK_LO = 16
V7X_VMEM_BYTES = 64 * 1024 * 1024
VMEM_BIG = 56 * 1024 * 1024
VMEM_MID = 40 * 1024 * 1024

_NT = (((1,), (1,)), ((), ()))
_TN = (((0,), (0,)), ((), ()))


def _params(n_axes, vmem):
    return pltpu.CompilerParams(dimension_semantics=("arbitrary",) * n_axes, vmem_limit_bytes=vmem)


def _silu(x):
    return x * jax.nn.sigmoid(x)


def _layernorm_rows(z, g, b):
    mu = jnp.mean(z, -1, keepdims=True)
    zc = z - mu
    var = jnp.mean(zc * zc, -1, keepdims=True)
    return zc * lax.rsqrt(var + LN_EPS) * g + b


def _ada_kernel(cs_ref, w_ref, b_ref, o_ref):
    s = _silu(cs_ref[...])
    o_ref[0] = jnp.dot(s, w_ref[0], preferred_element_type=F32, precision=HIGHEST) + b_ref[0]


def ada_modulation(cs, ada_w, ada_b):
    depth, D, D6 = ada_w.shape
    R = cs.shape[0]
    tn = D6 // 4
    return pl.pallas_call(
        _ada_kernel,
        grid=(depth, D6 // tn),
        in_specs=[pl.BlockSpec((R, D), lambda l, j: (0, 0)),
                  pl.BlockSpec((1, D, tn), lambda l, j: (l, 0, j)),
                  pl.BlockSpec((1, 1, tn), lambda l, j: (l, 0, j))],
        out_specs=pl.BlockSpec((1, R, tn), lambda l, j: (l, 0, j)),
        out_shape=jax.ShapeDtypeStruct((depth, R, D6), F32),
        compiler_params=_params(2, VMEM_MID),
        name="ada_modulation",
    )(cs, ada_w, ada_b.reshape(depth, 1, D6))


def _mm_mod_kernel(x_ref, sc_ref, sh_ref, w_ref, o_ref):
    h = x_ref[0] * (1.0 + sc_ref[0]) + sh_ref[0]
    o_ref[0] = jnp.dot(h.astype(BF16), w_ref[...], preferred_element_type=F32).astype(o_ref.dtype)


def mm_mod(x, sc, sh, w, out_dtype, tm):
    B, N, D = x.shape
    NO = w.shape[1]
    tm = min(tm, N)
    return pl.pallas_call(
        _mm_mod_kernel,
        grid=(B, N // tm),
        in_specs=[pl.BlockSpec((1, tm, D), lambda b, i: (b, i, 0)),
                  pl.BlockSpec((1, 1, D), lambda b, i: (b, 0, 0)),
                  pl.BlockSpec((1, 1, D), lambda b, i: (b, 0, 0)),
                  pl.BlockSpec((D, NO), lambda b, i: (0, 0), pipeline_mode=pl.Buffered(1))],
        out_specs=pl.BlockSpec((1, tm, NO), lambda b, i: (b, i, 0)),
        out_shape=jax.ShapeDtypeStruct((B, N, NO), out_dtype),
        compiler_params=_params(2, VMEM_MID),
        name="mm_mod",
    )(x, sc, sh, w)


def _natten_bias_table(rpb):
    H = rpb.shape[0]
    qc = np.arange(GRID_W)
    kc = np.arange(GRID_W)
    col_start = np.clip(qc - NA_KC // 2, 0, GRID_W - NA_KC)
    valid = (kc[None, :] >= col_start[:, None]) & (kc[None, :] < col_start[:, None] + NA_KC)
    dc = np.clip(kc[None, :] - qc[:, None], -(NA_KC - 1), NA_KC - 1) + NA_KC - 1
    tab = rpb.astype(F32)[:, :, dc] + jnp.where(jnp.asarray(valid), 0.0, NEG_INF).astype(F32)[None, None]
    variants = [tab[:, d0:d0 + NA_KR].transpose(0, 2, 1, 3).reshape(H, GRID_W, NA_KR * GRID_W)
                for d0 in range(NA_KR)]
    return jnp.stack(variants)


def _softmax_pv(s_list, v_list):
    m = functools.reduce(jnp.maximum, [jnp.max(s, -1, keepdims=True) for s in s_list])
    ps = [jnp.exp(s - m) for s in s_list]
    den = functools.reduce(lambda a, b: a + b, [jnp.sum(p, -1, keepdims=True) for p in ps])
    o = functools.reduce(lambda a, b: a + b,
                         [jnp.dot(p.astype(BF16), v, preferred_element_type=F32) for p, v in zip(ps, v_list)])
    return o / den


def _natten_kernel(q_ref, k_ref, v_ref, kc_ref, vc_ref, bias_ref, o_ref, *, rows, scale):
    r = pl.program_id(1)
    rs = jnp.clip(r - NA_KR // 2, 0, rows - NA_KR)
    start = pl.multiple_of(rs * GRID_W, GRID_W)
    win = pl.ds(start, NA_KR * GRID_W)
    lane = lax.broadcasted_iota(I32, (GRID_W, LANES), 1)
    per_tile = LANES // NA_HEAD_DIM
    masks = [(lane >= half * NA_HEAD_DIM) & (lane < (half + 1) * NA_HEAD_DIM) for half in range(per_tile)]
    tiles = [slice(pair * LANES, (pair + 1) * LANES) for pair in range(NA_HEADS // per_tile)]
    scores = []
    for pair, sl in enumerate(tiles):
        q2 = q_ref[0, :, sl] * scale
        for half in range(per_tile):
            qh = jnp.where(masks[half], q2, jnp.zeros_like(q2))
            s_w = lax.dot_general(qh, k_ref[0, win, sl], _NT, preferred_element_type=F32)
            s_c = lax.dot_general(qh, kc_ref[0, :, sl], _NT, preferred_element_type=F32)
            scores.append((s_w + bias_ref[0, pair * per_tile + half], s_c))
    maxes = [jnp.maximum(jnp.max(s_w, -1, keepdims=True), jnp.max(s_c, -1, keepdims=True)) for s_w, s_c in scores]
    probs = [(jnp.exp(s_w - m), jnp.exp(s_c - m)) for (s_w, s_c), m in zip(scores, maxes)]
    dens = [jnp.sum(p_w, -1, keepdims=True) + jnp.sum(p_c, -1, keepdims=True) for p_w, p_c in probs]
    for pair, sl in enumerate(tiles):
        o = None
        for half in range(per_tile):
            h = pair * per_tile + half
            p_w, p_c = probs[h]
            pv = (jnp.dot(p_w.astype(BF16), v_ref[0, win, sl], preferred_element_type=F32)
                  + jnp.dot(p_c.astype(BF16), vc_ref[0, :, sl], preferred_element_type=F32)) / dens[h]
            o = pv if o is None else jnp.where(masks[half], pv, o)
        o_ref[0, :, sl] = o.astype(o_ref.dtype)


def natten(qkv, qkvc, bias_tab):
    B, N, D3 = qkv.shape
    D = D3 // 3
    L = qkvc.shape[1]
    rows = N // GRID_W
    assert rows >= NA_KR and N % GRID_W == 0
    scale = NA_HEAD_DIM ** -0.5
    assert math.frexp(scale)[0] == 0.5, "q is pre-scaled in bf16: the scale must be a power of two"

    def bias_index(b, r):
        rs = jnp.clip(r - NA_KR // 2, 0, rows - NA_KR)
        return (rs - r + NA_KR - 1, 0, 0, 0)

    return pl.pallas_call(
        functools.partial(_natten_kernel, rows=rows, scale=scale),
        grid=(B, rows),
        in_specs=[pl.BlockSpec((1, GRID_W, D), lambda b, r: (b, r, 0)),
                  pl.BlockSpec((1, N, D), lambda b, r: (b, 0, 1)),
                  pl.BlockSpec((1, N, D), lambda b, r: (b, 0, 2)),
                  pl.BlockSpec((1, L, D), lambda b, r: (b, 0, 1)),
                  pl.BlockSpec((1, L, D), lambda b, r: (b, 0, 2)),
                  pl.BlockSpec((1, NA_HEADS, GRID_W, NA_KR * GRID_W), bias_index)],
        out_specs=pl.BlockSpec((1, GRID_W, D), lambda b, r: (b, r, 0)),
        out_shape=jax.ShapeDtypeStruct((B, N, D), BF16),
        compiler_params=_params(2, VMEM_BIG),
        name="natten",
    )(qkv, qkv, qkv, qkvc, qkvc, bias_tab)


def _ctx_attn_kernel(q_ref, k_ref, v_ref, o_ref, *, scale):
    for h in range(NA_HEADS):
        sl = slice(h * NA_HEAD_DIM, (h + 1) * NA_HEAD_DIM)
        s = lax.dot_general(q_ref[0, :, sl], k_ref[0, :, sl], _NT, preferred_element_type=F32) * scale
        o_ref[0, :, sl] = _softmax_pv([s], [v_ref[0, :, sl]]).astype(o_ref.dtype)


def ctx_attention(qkvc):
    B, L, D3 = qkvc.shape
    D = D3 // 3
    return pl.pallas_call(
        functools.partial(_ctx_attn_kernel, scale=NA_HEAD_DIM ** -0.5),
        grid=(B,),
        in_specs=[pl.BlockSpec((1, L, D), lambda b: (b, 0, 0)),
                  pl.BlockSpec((1, L, D), lambda b: (b, 0, 1)),
                  pl.BlockSpec((1, L, D), lambda b: (b, 0, 2))],
        out_specs=pl.BlockSpec((1, L, D), lambda b: (b, 0, 0)),
        out_shape=jax.ShapeDtypeStruct((B, L, D), BF16),
        compiler_params=_params(1, VMEM_MID),
        name="ctx_attention",
    )(qkvc, qkvc, qkvc)


def _is_first_step():
    return (pl.program_id(0) == 0) & (pl.program_id(1) == 0)


def _mm_ln_kernel(a_ref, x_ref, gate_ref, w_ref, g_ref, b_ref, o_ref, wb_ref, *, alpha):
    @pl.when(_is_first_step())
    def _():
        wb_ref[...] = w_ref[...].astype(BF16)

    y = jnp.dot(a_ref[0], wb_ref[...], preferred_element_type=F32)
    o_ref[0] = _layernorm_rows(alpha * x_ref[0] + gate_ref[0] * y, g_ref[...], b_ref[...])


def mm_ln(a, x, gate, w, g, b, alpha, tm):
    B, N, D = x.shape
    tm = min(tm, N)
    row = lambda bi, i: (bi, i, 0)
    vec = lambda bi, i: (0, 0)
    return pl.pallas_call(
        functools.partial(_mm_ln_kernel, alpha=alpha),
        grid=(B, N // tm),
        in_specs=[pl.BlockSpec((1, tm, D), row),
                  pl.BlockSpec((1, tm, D), row),
                  pl.BlockSpec((1, 1, D), lambda bi, i: (bi, 0, 0)),
                  pl.BlockSpec((D, D), vec),
                  pl.BlockSpec((1, D), vec),
                  pl.BlockSpec((1, D), vec)],
        out_specs=pl.BlockSpec((1, tm, D), row),
        out_shape=jax.ShapeDtypeStruct((B, N, D), F32),
        scratch_shapes=[pltpu.VMEM((D, D), BF16)],
        compiler_params=_params(2, VMEM_MID),
        name="mm_ln",
    )(a, x, gate, w, g.reshape(1, D), b.reshape(1, D))


def _mm_ln_gdn_kernel(of_ref, ob_ref, z_ref, nw_ref, x_ref, gate_ref, w_ref, g_ref, b_ref, o_ref,
                      wb_ref, a_ref, *, alpha):
    @pl.when(_is_first_step())
    def _():
        wb_ref[...] = w_ref[...].astype(BF16)

    for h in range(GDN_HEADS):
        sl = slice(h * GDN_HEAD_DIM, (h + 1) * GDN_HEAD_DIM)
        o = of_ref[0, :, sl] + ob_ref[0, :, sl]
        y = o * lax.rsqrt(jnp.mean(o * o, -1, keepdims=True) + LN_EPS) * nw_ref[...]
        a_ref[:, sl] = (y * _silu(z_ref[0, :, sl])).astype(BF16)
    y = jnp.dot(a_ref[...], wb_ref[...], preferred_element_type=F32)
    o_ref[0] = _layernorm_rows(alpha * x_ref[0] + gate_ref[0] * y, g_ref[...], b_ref[...])


def mm_ln_gdn(o_f, o_b, p, norm_w, x, gate, w, g, b, alpha, tm):
    B, N, D = x.shape
    tm = min(tm, N)
    row = lambda bi, i: (bi, i, 0)
    vec = lambda bi, i: (0, 0)
    return pl.pallas_call(
        functools.partial(_mm_ln_gdn_kernel, alpha=alpha),
        grid=(B, N // tm),
        in_specs=[pl.BlockSpec((1, tm, D), row),
                  pl.BlockSpec((1, tm, D), row),
                  pl.BlockSpec((1, tm, D), lambda bi, i: (bi, i, 3)),
                  pl.BlockSpec((1, GDN_HEAD_DIM), vec),
                  pl.BlockSpec((1, tm, D), row),
                  pl.BlockSpec((1, 1, D), lambda bi, i: (bi, 0, 0)),
                  pl.BlockSpec((D, D), vec),
                  pl.BlockSpec((1, D), vec),
                  pl.BlockSpec((1, D), vec)],
        out_specs=pl.BlockSpec((1, tm, D), row),
        out_shape=jax.ShapeDtypeStruct((B, N, D), F32),
        scratch_shapes=[pltpu.VMEM((D, D), BF16), pltpu.VMEM((tm, D), BF16)],
        compiler_params=_params(2, VMEM_MID),
        name="mm_ln_gdn",
    )(o_f, o_b, p, norm_w.reshape(1, GDN_HEAD_DIM), x, gate, w, g.reshape(1, D), b.reshape(1, D))


def _conv_kernel(prev_ref, main_ref, next_ref, w_ref, o_ref, ext_ref, *, tt, nt):
    i = pl.program_id(1)
    j = pl.program_id(2)
    ext_ref[0:CONV_HALO, :] = jnp.where(i > 0, prev_ref[0], 0.0)
    ext_ref[CONV_HALO:CONV_HALO + tt, :] = main_ref[0]
    ext_ref[CONV_HALO + tt:, :] = jnp.where(i < nt - 1, next_ref[0], 0.0)
    pad = GDN_CONV // 2
    acc = w_ref[0:1, :] * ext_ref[CONV_HALO - pad:CONV_HALO - pad + tt, :]
    for t in range(1, GDN_CONV):
        acc = acc + w_ref[t:t + 1, :] * ext_ref[CONV_HALO - pad + t:CONV_HALO - pad + t + tt, :]
    y = _silu(acc)
    qscale = jnp.where(j == 0, GDN_HEAD_DIM ** -0.5, 1.0)
    for h in range(GDN_HEADS):
        sl = slice(h * GDN_HEAD_DIM, (h + 1) * GDN_HEAD_DIM)
        seg = y[:, sl]
        inv = lax.rsqrt(jnp.sum(seg * seg, -1, keepdims=True) + 1e-6) * qscale
        o_ref[0, :, sl] = seg * jnp.where(j < 2, inv, 1.0)


def gdn_conv(p, conv_w, tt):
    B, T = p.shape[:2]
    D = conv_w.shape[1] // 3
    tt = min(tt, T)
    nt = T // tt
    hb = tt // CONV_HALO
    return pl.pallas_call(
        functools.partial(_conv_kernel, tt=tt, nt=nt),
        grid=(B, nt, 3),
        in_specs=[pl.BlockSpec((1, CONV_HALO, D), lambda b, i, j: (b, jnp.maximum(i * hb - 1, 0), j)),
                  pl.BlockSpec((1, tt, D), lambda b, i, j: (b, i, j)),
                  pl.BlockSpec((1, CONV_HALO, D), lambda b, i, j: (b, jnp.minimum((i + 1) * hb, T // CONV_HALO - 1), j)),
                  pl.BlockSpec((GDN_CONV, D), lambda b, i, j: (0, j))],
        out_specs=pl.BlockSpec((1, tt, D), lambda b, i, j: (b, i, j)),
        out_shape=jax.ShapeDtypeStruct((B, T, 3 * D), F32),
        scratch_shapes=[pltpu.VMEM((tt + 2 * CONV_HALO, D), F32)],
        compiler_params=_params(3, VMEM_MID),
        name="gdn_conv",
    )(p, p, p, conv_w)


def _gates_kernel(ab_ref, alog_ref, dtb_ref, o_ref):
    ab = ab_ref[0]
    xg = ab + dtb_ref[...]
    softplus = jnp.maximum(xg, 0.0) + jnp.log(1.0 + jnp.exp(-jnp.abs(xg)))
    g = -jnp.exp(alog_ref[...]) * softplus
    beta = jax.nn.sigmoid(ab)
    lane = lax.broadcasted_iota(I32, ab.shape, 1)
    o_ref[0] = jnp.where(lane < 2 * GDN_HEADS, g, beta)


def gdn_gates(ab, a_log, dt_bias, tt):
    B, T, W = ab.shape
    tt = min(tt, T)
    pad = lambda a: jnp.pad(a.reshape(1, -1).astype(F32), ((0, 0), (0, W - a.size)))
    return pl.pallas_call(
        _gates_kernel,
        grid=(B, T // tt),
        in_specs=[pl.BlockSpec((1, tt, W), lambda b, i: (b, i, 0)),
                  pl.BlockSpec((1, W), lambda b, i: (0, 0)),
                  pl.BlockSpec((1, W), lambda b, i: (0, 0))],
        out_specs=pl.BlockSpec((1, tt, W), lambda b, i: (b, i, 0)),
        out_shape=jax.ShapeDtypeStruct((B, T, W), F32),
        compiler_params=_params(2, VMEM_MID),
        name="gdn_gates",
    )(ab, pad(a_log), pad(dt_bias))


TRI_BASE = 16


def _bmm(a, b):
    return jnp.einsum("hij,hjk->hik", a.astype(BF16), b.astype(BF16), preferred_element_type=F32)


def _unit_triangular_inverse(A, eye):
    C = A.shape[-1]
    r = lax.broadcasted_iota(I32, (C, C), 0)
    c = lax.broadcasted_iota(I32, (C, C), 1)
    sh = int(math.log2(TRI_BASE))
    B = jnp.where((r >> sh) == (c >> sh), A, 0.0)
    T = eye - B
    P = B
    for _ in range(sh - 1):
        P = _bmm(P, P)
        T = T + _bmm(T, P)
    while sh < int(math.log2(C)):
        off = ((r >> (sh + 1)) == (c >> (sh + 1))) & ((r >> sh) != (c >> sh))
        T = T - _bmm(_bmm(T, jnp.where(off, A, 0.0)), T)
        sh += 1
    return T


def _gdn_chunk(d, q_ref, k_ref, v_ref, gb, Gc_all, Gr_all, S_ref, o_ref, incl, strict, eye):
    C, H, dk = GDN_CHUNK, GDN_HEADS, GDN_HEAD_DIM
    last = C - 1 if d == 0 else 0
    cols = [d * H + h for h in range(H)]
    heads = lambda f: jnp.stack([f(h, cols[h]) for h in range(H)])
    Gc = heads(lambda h, col: Gc_all[:, col:col + 1])
    Gr = heads(lambda h, col: Gr_all[col:col + 1, :])
    Gtot = heads(lambda h, col: Gc_all[last:last + 1, col:col + 1])
    beta = heads(lambda h, col: gb[:, 2 * H + col:2 * H + col + 1])
    q = heads(lambda h, col: q_ref[0, :, h * dk:(h + 1) * dk])
    k = heads(lambda h, col: k_ref[0, :, h * dk:(h + 1) * dk])
    v = heads(lambda h, col: v_ref[0, :, h * dk:(h + 1) * dk])
    decay = jnp.where(incl, jnp.exp(jnp.minimum(Gc - Gr, 0.0)), 0.0)
    kb = k * beta
    kq = jnp.einsum("hid,hjd->hij", jnp.concatenate([kb, q], axis=1).astype(BF16), k.astype(BF16),
                    preferred_element_type=F32)
    A = jnp.where(strict, kq[:, :C] * decay, 0.0)
    attn = kq[:, C:] * decay
    T = _unit_triangular_inverse(A, eye)
    eG = jnp.exp(Gc)
    wu = _bmm(T, jnp.concatenate([kb * eG, v * beta], axis=2))
    S = S_ref[d]
    wq = _bmm(jnp.concatenate([wu[:, :, :dk], q * eG], axis=1), S)
    v_new = wu[:, :, dk:] - wq[:, :C]
    o = wq[:, C:] + _bmm(attn, v_new)
    k_tail = k * jnp.exp(Gtot - Gc)
    S_ref[d] = S * jnp.exp(Gtot) + jnp.einsum("hcd,hce->hde", k_tail.astype(BF16), v_new.astype(BF16),
                                              preferred_element_type=F32)
    for h in range(H):
        o_ref[0, :, h * dk:(h + 1) * dk] = o[h]


def _gdn_scan_kernel(qf_ref, kf_ref, vf_ref, gbf_ref, gtf_ref, qb_ref, kb_ref, vb_ref, gbb_ref, gtb_ref,
                     s0_ref, of_ref, ob_ref, sout_ref, S_ref, *, nsteps):
    i = pl.program_id(1)

    @pl.when(i == 0)
    def _():
        S_ref[...] = s0_ref[0]

    C = GDN_CHUNK
    r = lax.broadcasted_iota(I32, (C, C), 0)
    c = lax.broadcasted_iota(I32, (C, C), 1)
    eye = jnp.where(r == c, 1.0, 0.0).astype(F32)
    streams = ((qf_ref, kf_ref, vf_ref, gbf_ref, gtf_ref, of_ref, r >= c, r > c),
               (qb_ref, kb_ref, vb_ref, gbb_ref, gtb_ref, ob_ref, r <= c, r < c))
    for d, (q_ref, k_ref, v_ref, gb_ref, gt_ref, o_ref, incl, strict) in enumerate(streams):
        tri = jnp.where(incl, 1.0, 0.0).astype(F32)
        gb = gb_ref[0]
        Gc_all = jnp.dot(tri, gb, preferred_element_type=F32, precision=HIGHEST)
        Gr_all = lax.dot_general(gt_ref[0], tri, _NT, preferred_element_type=F32, precision=HIGHEST)
        _gdn_chunk(d, q_ref, k_ref, v_ref, gb, Gc_all, Gr_all, S_ref, o_ref, incl, strict, eye)

    @pl.when(i == nsteps - 1)
    def _():
        sout_ref[0] = S_ref[...]


def gdn_scan(qkv, gb, gbt, s0):
    B, T, D3 = qkv.shape
    D = D3 // 3
    C = GDN_CHUNK
    n = T // C
    W = gb.shape[2]
    R = gbt.shape[1]
    fwd = lambda j: (lambda b, i: (b, i, j))
    bwd = lambda j: (lambda b, i: (b, n - 1 - i, j))
    st = lambda b, i: (b, 0, 0, 0, 0)
    sspec = pl.BlockSpec((1, 2, GDN_HEADS, GDN_HEAD_DIM, GDN_HEAD_DIM), st)
    return pl.pallas_call(
        functools.partial(_gdn_scan_kernel, nsteps=n),
        grid=(B, n),
        in_specs=[pl.BlockSpec((1, C, D), fwd(0)), pl.BlockSpec((1, C, D), fwd(1)), pl.BlockSpec((1, C, D), fwd(2)),
                  pl.BlockSpec((1, C, W), fwd(0)), pl.BlockSpec((1, R, C), lambda b, i: (b, 0, i)),
                  pl.BlockSpec((1, C, D), bwd(0)), pl.BlockSpec((1, C, D), bwd(1)), pl.BlockSpec((1, C, D), bwd(2)),
                  pl.BlockSpec((1, C, W), bwd(0)), pl.BlockSpec((1, R, C), lambda b, i: (b, 0, n - 1 - i)),
                  sspec],
        out_specs=[pl.BlockSpec((1, C, D), fwd(0)), pl.BlockSpec((1, C, D), bwd(0)), sspec],
        out_shape=[jax.ShapeDtypeStruct((B, T, D), F32), jax.ShapeDtypeStruct((B, T, D), F32),
                   jax.ShapeDtypeStruct(s0.shape, F32)],
        scratch_shapes=[pltpu.VMEM((2, GDN_HEADS, GDN_HEAD_DIM, GDN_HEAD_DIM), F32)],
        compiler_params=_params(2, VMEM_MID),
        name="gdn_scan",
    )(qkv, qkv, qkv, gb, gbt, qkv, qkv, qkv, gb, gbt, s0)


def _router_kernel(x_ref, sc_ref, sh_ref, wr_ref, o_ref):
    h = x_ref[0] * (1.0 + sc_ref[0]) + sh_ref[0]
    lg = lax.dot_general(wr_ref[...], h, _NT, preferred_element_type=F32, precision=HIGHEST)
    e = jnp.exp(lg - jnp.max(lg, 0, keepdims=True))
    o_ref[0] = e / jnp.sum(e, 0, keepdims=True)


def router(x, sc, sh, w_router_t, tt):
    B, n, D = x.shape
    E = w_router_t.shape[0]
    tt = min(tt, n)
    return pl.pallas_call(
        _router_kernel,
        grid=(B, n // tt),
        in_specs=[pl.BlockSpec((1, tt, D), lambda b, i: (b, i, 0)),
                  pl.BlockSpec((1, 1, D), lambda b, i: (b, 0, 0)),
                  pl.BlockSpec((1, 1, D), lambda b, i: (b, 0, 0)),
                  pl.BlockSpec((E, D), lambda b, i: (0, 0))],
        out_specs=pl.BlockSpec((1, E, tt), lambda b, i: (b, 0, i)),
        out_shape=jax.ShapeDtypeStruct((B, E, n), F32),
        compiler_params=_params(2, VMEM_MID),
        name="router",
    )(x, sc, sh, w_router_t)


def _exclusive_cumsum_lanes(mask, n):
    r = lax.broadcasted_iota(I32, (LANES, LANES), 0)
    c = lax.broadcasted_iota(I32, (LANES, LANES), 1)
    upper = jnp.where(r < c, 1.0, 0.0).astype(BF16)
    carry = jnp.zeros((mask.shape[0], 1), F32)
    out = []
    for j in range(n // LANES):
        blk = mask[:, j * LANES:(j + 1) * LANES]
        out.append(jnp.dot(blk.astype(BF16), upper, preferred_element_type=F32) + carry)
        carry = carry + jnp.sum(blk, -1, keepdims=True)
    return jnp.concatenate(out, axis=1)


def _topk_kernel(aff_ref, idx_ref, gate_ref, hi_ref, lo_ref, pieces_ref, *, n, cap, n_hi, n_hi_pad):
    E = aff_ref.shape[1]
    aff = aff_ref[0]
    bits = pltpu.bitcast(aff, I32)
    count = lambda m: jnp.sum(jnp.where(m, 1.0, 0.0), -1, keepdims=True)
    t = jnp.zeros((E, 1), I32)
    for bit in range(30, -1, -1):
        cand = t | (1 << bit)
        t = jnp.where(count(bits >= cand) >= cap, cand, t)
    gt = bits > t
    eq = bits == t
    need = cap - count(gt)
    eq_rank = _exclusive_cumsum_lanes(jnp.where(eq, 1.0, 0.0), n)
    sel = gt | (eq & (eq_rank < need))
    pos = _exclusive_cumsum_lanes(jnp.where(sel, 1.0, 0.0), n).astype(I32)
    hi_ref[...] = jnp.where(sel, pos >> int(math.log2(TOPK_LO)), -1)
    lo_ref[...] = pos & (TOPK_LO - 1)
    g_hi = aff.astype(BF16).astype(F32)
    g_mid = (aff - g_hi).astype(BF16).astype(F32)
    pieces_ref[0 * E:1 * E, :] = g_hi
    pieces_ref[1 * E:2 * E, :] = g_mid
    pieces_ref[2 * E:3 * E, :] = aff - g_hi - g_mid
    tok = lax.broadcasted_iota(I32, (1, n), 1)
    tok_hi = (tok >> 6).astype(F32)
    tok_lo = (tok & 63).astype(F32)
    hi_iota = lax.broadcasted_iota(I32, (n_hi_pad, n), 0)
    lo_iota = lax.broadcasted_iota(I32, (TOPK_LO, n), 0)

    def per_expert(e, carry):
        a = jnp.where(hi_ref[pl.ds(e, 1), :] == hi_iota, 1.0, 0.0).astype(BF16)
        b = jnp.where(lo_ref[pl.ds(e, 1), :] == lo_iota, 1.0, 0.0)
        vals = [tok_hi, tok_lo] + [pieces_ref[pl.ds(k * E + e, 1), :] for k in range(3)]
        rhs = jnp.concatenate([(b * v).astype(BF16) for v in vals], axis=0)
        res = lax.dot_general(a, rhs, _NT, preferred_element_type=F32)
        part = lambda k: res[:n_hi, k * TOPK_LO:(k + 1) * TOPK_LO]
        idx_ref[0, pl.ds(e, 1)] = (part(0) * 64.0 + part(1)).astype(I32)[None]
        gate_ref[0, pl.ds(e, 1)] = (part(2) + part(3) + part(4))[None]
        return carry

    lax.fori_loop(0, E, per_expert, 0)


def topk_select(aff, cap):
    B, E, n = aff.shape
    R = B * E
    n_hi = cap // TOPK_LO
    n_hi_pad = -(-n_hi // SUBLANES) * SUBLANES
    idx, gate = pl.pallas_call(
        functools.partial(_topk_kernel, n=n, cap=cap, n_hi=n_hi, n_hi_pad=n_hi_pad),
        grid=(1,),
        in_specs=[pl.BlockSpec((1, R, n), lambda b: (0, 0, 0))],
        out_specs=[pl.BlockSpec((1, R, n_hi, TOPK_LO), lambda b: (0, 0, 0, 0)),
                   pl.BlockSpec((1, R, n_hi, TOPK_LO), lambda b: (0, 0, 0, 0))],
        out_shape=[jax.ShapeDtypeStruct((1, R, n_hi, TOPK_LO), I32), jax.ShapeDtypeStruct((1, R, n_hi, TOPK_LO), F32)],
        scratch_shapes=[pltpu.VMEM((R, n), I32), pltpu.VMEM((R, n), I32), pltpu.VMEM((3 * R, n), F32)],
        compiler_params=_params(1, VMEM_MID),
        name="topk_select",
    )(aff.reshape(1, R, n))
    return idx.reshape(B, E, cap), gate.reshape(B, E, cap)


def _moe_kernel(idx_ref, gate_ref, x_ref, sc_ref, sh_ref, og_ref, lng_ref, lnb_ref, wg_ref, wu_ref, wd_ref, y_ref,
                xs32_ref, xs_ref, ye_ref, yp_ref, *, S, nfc, alpha):
    e = pl.program_id(0)
    f = pl.program_id(1)
    E = pl.num_programs(0)
    per_step = S // nfc // SUBLANES

    def gather8(expert, base):
        for j in range(SUBLANES):
            i = idx_ref[expert * S + base + j]
            xs32_ref[pl.ds(base + j, 1), :] = x_ref[0, pl.ds(i, 1), :]

    def scatter8(expert, base, scale):
        rows, news = [], []
        for j in range(SUBLANES):
            i = idx_ref[expert * S + base + j]
            g = gate_ref[expert * S + base + j] * scale
            rows.append(pl.ds(i, 1))
            news.append(y_ref[0, pl.ds(i, 1), :] + g * yp_ref[pl.ds(base + j, 1), :])
        for r, new in zip(rows, news):
            y_ref[0, r, :] = new

    def modulated_bf16():
        return (xs32_ref[...] * (1.0 + sc_ref[0]) + sh_ref[0]).astype(BF16)

    @pl.when((e == 0) & (f == 0))
    def _():
        y_ref[0] = jnp.zeros(y_ref.shape[1:], F32)
        yp_ref[...] = jnp.zeros(yp_ref.shape, F32)

        def body(s8, carry):
            gather8(0, pl.multiple_of(s8 * SUBLANES, SUBLANES))
            return carry

        lax.fori_loop(0, S // SUBLANES, body, 0)
        xs_ref[...] = modulated_bf16()

    prev = jnp.maximum(e - 1, 0)
    prev_scale = jnp.where(e > 0, 1.0, 0.0)
    nxt = jnp.minimum(e + 1, E - 1)
    def row_ops(part_index, n_parts):
        for s8 in range(part_index, per_step, n_parts):
            base = pl.multiple_of((f * per_step + s8) * SUBLANES, SUBLANES)
            scatter8(prev, base, prev_scale)
            gather8(nxt, base)

    xs = xs_ref[...]
    row_ops(0, 3)
    a = jnp.dot(xs, wg_ref[0], preferred_element_type=F32)
    row_ops(1, 3)
    u = jnp.dot(xs, wu_ref[0], preferred_element_type=F32)
    row_ops(2, 3)
    part = jnp.dot((_silu(a) * u).astype(BF16), wd_ref[0], preferred_element_type=F32)

    @pl.when(f == 0)
    def _():
        ye_ref[...] = part

    @pl.when(f > 0)
    def _():
        ye_ref[...] += part

    @pl.when(f == nfc - 1)
    def _():
        yp_ref[...] = ye_ref[...]
        xs_ref[...] = modulated_bf16()

    @pl.when((e == E - 1) & (f == nfc - 1))
    def _():
        def body(s8, carry):
            scatter8(e, pl.multiple_of(s8 * SUBLANES, SUBLANES), 1.0)
            return carry

        lax.fori_loop(0, S // SUBLANES, body, 0)

        def ln_body(t, carry):
            rows = pl.ds(pl.multiple_of(t * MOE_LN_ROWS, MOE_LN_ROWS), MOE_LN_ROWS)
            z = alpha * x_ref[0, rows, :] + og_ref[0] * y_ref[0, rows, :]
            y_ref[0, rows, :] = _layernorm_rows(z, lng_ref[...], lnb_ref[...])
            return carry

        lax.fori_loop(0, y_ref.shape[1] // MOE_LN_ROWS, ln_body, 0)


def moe_experts(idx, gate, x, g, sc, sh, out_gate, ln_g, ln_b, wg, wu, wd, e0, n_experts, fc, alpha):
    G, R, D = x.shape
    F = wg.shape[2]
    E = n_experts
    S = idx.shape[0] // E
    nfc = F // fc
    grp = lambda e, f, *_: (g, 0, 0)
    vec = lambda e, f, *_: (0, 0)
    return pl.pallas_call(
        functools.partial(_moe_kernel, S=S, nfc=nfc, alpha=alpha),
        grid_spec=pltpu.PrefetchScalarGridSpec(
            num_scalar_prefetch=2,
            grid=(E, nfc),
            in_specs=[pl.BlockSpec((1, R, D), grp, pipeline_mode=pl.Buffered(1)),
                      pl.BlockSpec((1, 1, D), grp),
                      pl.BlockSpec((1, 1, D), grp),
                      pl.BlockSpec((1, 1, D), grp),
                      pl.BlockSpec((1, D), vec),
                      pl.BlockSpec((1, D), vec),
                      pl.BlockSpec((1, D, fc), lambda e, f, *_: (e0 + e, 0, f)),
                      pl.BlockSpec((1, D, fc), lambda e, f, *_: (e0 + e, 0, f)),
                      pl.BlockSpec((1, fc, D), lambda e, f, *_: (e0 + e, f, 0))],
            out_specs=pl.BlockSpec((1, R, D), grp, pipeline_mode=pl.Buffered(1)),
            scratch_shapes=[pltpu.VMEM((S, D), F32), pltpu.VMEM((S, D), BF16), pltpu.VMEM((S, D), F32),
                            pltpu.VMEM((S, D), F32)]),
        out_shape=jax.ShapeDtypeStruct((G, R, D), F32),
        input_output_aliases={2: 0},
        compiler_params=_params(2, VMEM_BIG),
        name="moe_experts",
    )(idx, gate, x, sc, sh, out_gate, ln_g.reshape(1, D), ln_b.reshape(1, D), wg, wu, wd)


def ec_moe_residual(x, sc, sh, out_gate, ln_g, ln_b, w_router_t, wg, wu, wd, e0, alpha, flatten_groups):
    B, n, D = x.shape
    E = w_router_t.shape[0]
    cap = EC_CAPACITY_FACTOR * n // E
    aff = router(x, sc, sh, w_router_t, 512)
    idx, gate = topk_select(aff, cap)
    fc = min(MOE_HIDDEN_CHUNK, wg.shape[2])
    if flatten_groups:
        idx = idx + (jnp.arange(B, dtype=I32) * n)[:, None, None]
        idx = idx.transpose(1, 0, 2).reshape(-1)
        gate = gate.transpose(1, 0, 2).reshape(-1)
        x = moe_experts(idx, gate, x.reshape(1, B * n, D), 0, sc[:1], sh[:1], out_gate[:1], ln_g, ln_b,
                        wg, wu, wd, e0, E, fc, alpha)
        return x.reshape(B, n, D)
    for b in range(B):
        x = moe_experts(idx[b].reshape(-1), gate[b].reshape(-1), x, b, sc, sh, out_gate, ln_g, ln_b,
                        wg, wu, wd, e0, E, fc, alpha)
    return x


def kernel(x, c, ctx, c_ctx, ada_w, ada_b, ln_g, ln_b, na_w_qkv, na_w_o, na_rpb, gdn_w_in, gdn_conv_w,
           gdn_a_log, gdn_dt_bias, gdn_norm_w, gdn_w_o, moe_w_router, moe_w_gate, moe_w_up, moe_w_down):
    B, N, D = x.shape
    L = ctx.shape[1]
    depth = ada_w.shape[0]
    alpha = (2.0 * depth) ** 0.25
    xc = ctx

    rows = -(-(B + 1) // SUBLANES) * SUBLANES
    cs = jnp.zeros((rows, D), F32).at[:B].set(c).at[B].set(c_ctx)
    mods = ada_modulation(cs, ada_w, ada_b)

    E = moe_w_gate.shape[1]
    stack = lambda w: w.astype(BF16).reshape((depth * E,) + w.shape[2:])
    wg_all, wu_all, wd_all = stack(moe_w_gate), stack(moe_w_up), stack(moe_w_down)

    for l in range(depth):
        last = l == depth - 1
        i = l // 2
        mod = [mods[l, :B, j * D:(j + 1) * D].reshape(B, 1, D) for j in range(6)]
        modc = [jnp.broadcast_to(mods[l, B, j * D:(j + 1) * D].reshape(1, 1, D), (B, 1, D)) for j in range(6)]
        g1, b1, g2, b2 = ln_g[l, 0], ln_b[l, 0], ln_g[l, 1], ln_b[l, 1]
        if l % 2 == 0:
            w_qkv = na_w_qkv[i].astype(BF16)
            qkv = mm_mod(x, mod[1], mod[0], w_qkv, BF16, 512)
            qkvc = mm_mod(xc, modc[1], modc[0], w_qkv, BF16, 512)
            o = natten(qkv, qkvc, _natten_bias_table(na_rpb[i]))
            x = mm_ln(o, x, mod[2], na_w_o[i], g1, b1, alpha, 512)
            if not last:
                xc = mm_ln(ctx_attention(qkvc), xc, modc[2], na_w_o[i], g1, b1, alpha, 512)
        else:
            w_main = gdn_w_in[i][:, :4 * D].astype(BF16)
            w_ab = jnp.pad(gdn_w_in[i][:, 4 * D:], ((0, 0), (0, LANES - 4 * GDN_HEADS))).astype(BF16)

            def project(u, m):
                p = mm_mod(u, m[1], m[0], w_main, F32, 512)
                ab = mm_mod(u, m[1], m[0], w_ab, F32, 1024)
                gb = gdn_gates(ab, gdn_a_log[i], gdn_dt_bias[i], 1024)
                gbt = jnp.swapaxes(gb[:, :, :4 * GDN_HEADS], 1, 2)
                return p, gdn_conv(p, gdn_conv_w[i], 512), gb, gbt

            p, qkv, gb, gbt = project(x, mod)
            pc, qkvc, gbc, gbtc = project(xc, modc)
            s0 = jnp.zeros((B, 2, GDN_HEADS, GDN_HEAD_DIM, GDN_HEAD_DIM), F32)
            oc_f, oc_b, s_ctx = gdn_scan(qkvc, gbc, gbtc, s0)
            o_f, o_b, _ = gdn_scan(qkv, gb, gbt, s_ctx)
            x = mm_ln_gdn(o_f, o_b, p, gdn_norm_w[i], x, mod[2], gdn_w_o[i], g1, b1, alpha, 512)
            if not last:
                xc = mm_ln_gdn(oc_f, oc_b, pc, gdn_norm_w[i], xc, modc[2], gdn_w_o[i], g1, b1, alpha, 512)
        wr_t = moe_w_router[l].T
        x = ec_moe_residual(x, mod[4], mod[3], mod[5], g2, b2, wr_t, wg_all, wu_all, wd_all, l * E, alpha, False)
        if not last:
            xc = ec_moe_residual(xc, modc[4], modc[3], modc[5], g2, b2, wr_t, wg_all, wu_all, wd_all, l * E, alpha,
                                 True)
    return x
```

```python
import functools
import math

import numpy as np
import jax
import jax.numpy as jnp
from jax import lax
from jax.experimental import pallas as pl
from jax.experimental.pallas import tpu as pltpu

F32 = jnp.float32
BF16 = jnp.bfloat16
I32 = jnp.int32
HIGHEST = lax.Precision.HIGHEST

GRID_W = 64
NA_HEADS = 16
NA_HEAD_DIM = 64
NA_KR = 8
NA_KC = 16
GDN_HEADS = 8
GDN_HEAD_DIM = 128
GDN_CONV = 5
N_EXPERTS = 16
EC_CAPACITY_FACTOR = 2
LN_EPS = 1e-6
NEG_INF = -1e30

GDN_CHUNK = 128
CONV_HALO = 8
SUBLANES = 8
LANES = 128
MOE_HIDDEN_CHUNK = 1024
MOE_LN_ROWS = 256
TOPK_LO = 16
TOPK_ROW_UNROLL = 8
V7X_VMEM_BYTES = 64 * 1024 * 1024
VMEM_BIG = 56 * 1024 * 1024
VMEM_MID = 40 * 1024 * 1024

_NT = (((1,), (1,)), ((), ()))
_TN = (((0,), (0,)), ((), ()))


def _params(n_axes, vmem):
    return pltpu.CompilerParams(dimension_semantics=("arbitrary",) * n_axes, vmem_limit_bytes=vmem)


def _silu(x):
    return x * jax.nn.sigmoid(x)


def _layernorm_rows(z, g, b):
    mu = jnp.mean(z, -1, keepdims=True)
    zc = z - mu
    var = jnp.mean(zc * zc, -1, keepdims=True)
    return zc * lax.rsqrt(var + LN_EPS) * g + b


def _ada_kernel(cs_ref, w_ref, b_ref, o_ref):
    s = _silu(cs_ref[...])
    o_ref[0] = jnp.dot(s, w_ref[0], preferred_element_type=F32, precision=HIGHEST) + b_ref[0]


def ada_modulation(cs, ada_w, ada_b):
    depth, D, D6 = ada_w.shape
    R = cs.shape[0]
    tn = D6 // 4
    return pl.pallas_call(
        _ada_kernel,
        grid=(depth, D6 // tn),
        in_specs=[pl.BlockSpec((R, D), lambda l, j: (0, 0)),
                  pl.BlockSpec((1, D, tn), lambda l, j: (l, 0, j)),
                  pl.BlockSpec((1, 1, tn), lambda l, j: (l, 0, j))],
        out_specs=pl.BlockSpec((1, R, tn), lambda l, j: (l, 0, j)),
        out_shape=jax.ShapeDtypeStruct((depth, R, D6), F32),
        compiler_params=_params(2, VMEM_MID),
        name="ada_modulation",
    )(cs, ada_w, ada_b.reshape(depth, 1, D6))


def _mm_mod_kernel(x_ref, sc_ref, sh_ref, w_ref, o_ref):
    h = x_ref[0] * (1.0 + sc_ref[0]) + sh_ref[0]
    o_ref[0] = jnp.dot(h.astype(BF16), w_ref[...], preferred_element_type=F32).astype(o_ref.dtype)


def mm_mod(x, sc, sh, w, out_dtype, tm):
    B, N, D = x.shape
    NO = w.shape[1]
    tm = min(tm, N)
    return pl.pallas_call(
        _mm_mod_kernel,
        grid=(B, N // tm),
        in_specs=[pl.BlockSpec((1, tm, D), lambda b, i: (b, i, 0)),
                  pl.BlockSpec((1, 1, D), lambda b, i: (b, 0, 0)),
                  pl.BlockSpec((1, 1, D), lambda b, i: (b, 0, 0)),
                  pl.BlockSpec((D, NO), lambda b, i: (0, 0), pipeline_mode=pl.Buffered(1))],
        out_specs=pl.BlockSpec((1, tm, NO), lambda b, i: (b, i, 0)),
        out_shape=jax.ShapeDtypeStruct((B, N, NO), out_dtype),
        compiler_params=_params(2, VMEM_MID),
        name="mm_mod",
    )(x, sc, sh, w)


def _natten_bias_table(rpb):
    H = rpb.shape[0]
    qc = np.arange(GRID_W)
    kc = np.arange(GRID_W)
    col_start = np.clip(qc - NA_KC // 2, 0, GRID_W - NA_KC)
    valid = (kc[None, :] >= col_start[:, None]) & (kc[None, :] < col_start[:, None] + NA_KC)
    dc = np.clip(kc[None, :] - qc[:, None], -(NA_KC - 1), NA_KC - 1) + NA_KC - 1
    tab = rpb.astype(F32)[:, :, dc] + jnp.where(jnp.asarray(valid), 0.0, NEG_INF).astype(F32)[None, None]
    variants = [tab[:, d0:d0 + NA_KR].transpose(0, 2, 1, 3).reshape(H, GRID_W, NA_KR * GRID_W)
                for d0 in range(NA_KR)]
    return jnp.stack(variants)


def _softmax_pv(s_list, v_list):
    m = functools.reduce(jnp.maximum, [jnp.max(s, -1, keepdims=True) for s in s_list])
    ps = [jnp.exp(s - m) for s in s_list]
    den = functools.reduce(lambda a, b: a + b, [jnp.sum(p, -1, keepdims=True) for p in ps])
    o = functools.reduce(lambda a, b: a + b,
                         [jnp.dot(p.astype(BF16), v, preferred_element_type=F32) for p, v in zip(ps, v_list)])
    return o / den


def _natten_kernel(q_ref, k_ref, v_ref, kc_ref, vc_ref, bias_ref, o_ref, *, rows, scale):
    r = pl.program_id(1)
    rs = jnp.clip(r - NA_KR // 2, 0, rows - NA_KR)
    start = pl.multiple_of(rs * GRID_W, GRID_W)
    win = pl.ds(start, NA_KR * GRID_W)
    lane = lax.broadcasted_iota(I32, (GRID_W, LANES), 1)
    per_tile = LANES // NA_HEAD_DIM
    masks = [(lane >= half * NA_HEAD_DIM) & (lane < (half + 1) * NA_HEAD_DIM) for half in range(per_tile)]
    tiles = [slice(pair * LANES, (pair + 1) * LANES) for pair in range(NA_HEADS // per_tile)]
    scores = []
    for pair, sl in enumerate(tiles):
        q2 = q_ref[0, :, sl] * scale
        for half in range(per_tile):
            qh = jnp.where(masks[half], q2, jnp.zeros_like(q2))
            s_w = lax.dot_general(qh, k_ref[0, win, sl], _NT, preferred_element_type=F32)
            s_c = lax.dot_general(qh, kc_ref[0, :, sl], _NT, preferred_element_type=F32)
            scores.append((s_w + bias_ref[0, pair * per_tile + half], s_c))
    maxes = [jnp.maximum(jnp.max(s_w, -1, keepdims=True), jnp.max(s_c, -1, keepdims=True)) for s_w, s_c in scores]
    probs = [(jnp.exp(s_w - m), jnp.exp(s_c - m)) for (s_w, s_c), m in zip(scores, maxes)]
    dens = [jnp.sum(p_w, -1, keepdims=True) + jnp.sum(p_c, -1, keepdims=True) for p_w, p_c in probs]
    for pair, sl in enumerate(tiles):
        o = None
        for half in range(per_tile):
            h = pair * per_tile + half
            p_w, p_c = probs[h]
            pv = (jnp.dot(p_w.astype(BF16), v_ref[0, win, sl], preferred_element_type=F32)
                  + jnp.dot(p_c.astype(BF16), vc_ref[0, :, sl], preferred_element_type=F32)) / dens[h]
            o = pv if o is None else jnp.where(masks[half], pv, o)
        o_ref[0, :, sl] = o.astype(o_ref.dtype)


def natten(qkv, qkvc, bias_tab):
    B, N, D3 = qkv.shape
    D = D3 // 3
    L = qkvc.shape[1]
    rows = N // GRID_W
    assert rows >= NA_KR and N % GRID_W == 0
    scale = NA_HEAD_DIM ** -0.5
    assert math.frexp(scale)[0] == 0.5, "q is pre-scaled in bf16: the scale must be a power of two"

    def bias_index(b, r):
        rs = jnp.clip(r - NA_KR // 2, 0, rows - NA_KR)
        return (rs - r + NA_KR - 1, 0, 0, 0)

    return pl.pallas_call(
        functools.partial(_natten_kernel, rows=rows, scale=scale),
        grid=(B, rows),
        in_specs=[pl.BlockSpec((1, GRID_W, D), lambda b, r: (b, r, 0)),
                  pl.BlockSpec((1, N, D), lambda b, r: (b, 0, 1)),
                  pl.BlockSpec((1, N, D), lambda b, r: (b, 0, 2)),
                  pl.BlockSpec((1, L, D), lambda b, r: (b, 0, 1)),
                  pl.BlockSpec((1, L, D), lambda b, r: (b, 0, 2)),
                  pl.BlockSpec((1, NA_HEADS, GRID_W, NA_KR * GRID_W), bias_index)],
        out_specs=pl.BlockSpec((1, GRID_W, D), lambda b, r: (b, r, 0)),
        out_shape=jax.ShapeDtypeStruct((B, N, D), BF16),
        compiler_params=_params(2, VMEM_BIG),
        name="natten",
    )(qkv, qkv, qkv, qkvc, qkvc, bias_tab)


def _ctx_attn_kernel(q_ref, k_ref, v_ref, o_ref, *, scale):
    for h in range(NA_HEADS):
        sl = slice(h * NA_HEAD_DIM, (h + 1) * NA_HEAD_DIM)
        s = lax.dot_general(q_ref[0, :, sl], k_ref[0, :, sl], _NT, preferred_element_type=F32) * scale
        o_ref[0, :, sl] = _softmax_pv([s], [v_ref[0, :, sl]]).astype(o_ref.dtype)


def ctx_attention(qkvc):
    B, L, D3 = qkvc.shape
    D = D3 // 3
    return pl.pallas_call(
        functools.partial(_ctx_attn_kernel, scale=NA_HEAD_DIM ** -0.5),
        grid=(B,),
        in_specs=[pl.BlockSpec((1, L, D), lambda b: (b, 0, 0)),
                  pl.BlockSpec((1, L, D), lambda b: (b, 0, 1)),
                  pl.BlockSpec((1, L, D), lambda b: (b, 0, 2))],
        out_specs=pl.BlockSpec((1, L, D), lambda b: (b, 0, 0)),
        out_shape=jax.ShapeDtypeStruct((B, L, D), BF16),
        compiler_params=_params(1, VMEM_MID),
        name="ctx_attention",
    )(qkvc, qkvc, qkvc)


def _is_first_step():
    return (pl.program_id(0) == 0) & (pl.program_id(1) == 0)


def _mm_ln_kernel(a_ref, x_ref, gate_ref, w_ref, g_ref, b_ref, o_ref, wb_ref, *, alpha):
    @pl.when(_is_first_step())
    def _():
        wb_ref[...] = w_ref[...].astype(BF16)

    y = jnp.dot(a_ref[0], wb_ref[...], preferred_element_type=F32)
    o_ref[0] = _layernorm_rows(alpha * x_ref[0] + gate_ref[0] * y, g_ref[...], b_ref[...])


def mm_ln(a, x, gate, w, g, b, alpha, tm):
    B, N, D = x.shape
    tm = min(tm, N)
    row = lambda bi, i: (bi, i, 0)
    vec = lambda bi, i: (0, 0)
    return pl.pallas_call(
        functools.partial(_mm_ln_kernel, alpha=alpha),
        grid=(B, N // tm),
        in_specs=[pl.BlockSpec((1, tm, D), row),
                  pl.BlockSpec((1, tm, D), row),
                  pl.BlockSpec((1, 1, D), lambda bi, i: (bi, 0, 0)),
                  pl.BlockSpec((D, D), vec),
                  pl.BlockSpec((1, D), vec),
                  pl.BlockSpec((1, D), vec)],
        out_specs=pl.BlockSpec((1, tm, D), row),
        out_shape=jax.ShapeDtypeStruct((B, N, D), F32),
        scratch_shapes=[pltpu.VMEM((D, D), BF16)],
        compiler_params=_params(2, VMEM_MID),
        name="mm_ln",
    )(a, x, gate, w, g.reshape(1, D), b.reshape(1, D))


def _mm_ln_gdn_kernel(of_ref, ob_ref, z_ref, nw_ref, x_ref, gate_ref, w_ref, g_ref, b_ref, o_ref,
                      wb_ref, a_ref, *, alpha):
    @pl.when(_is_first_step())
    def _():
        wb_ref[...] = w_ref[...].astype(BF16)

    for h in range(GDN_HEADS):
        sl = slice(h * GDN_HEAD_DIM, (h + 1) * GDN_HEAD_DIM)
        o = of_ref[0, :, sl] + ob_ref[0, :, sl]
        y = o * lax.rsqrt(jnp.mean(o * o, -1, keepdims=True) + LN_EPS) * nw_ref[...]
        a_ref[:, sl] = (y * _silu(z_ref[0, :, sl])).astype(BF16)
    y = jnp.dot(a_ref[...], wb_ref[...], preferred_element_type=F32)
    o_ref[0] = _layernorm_rows(alpha * x_ref[0] + gate_ref[0] * y, g_ref[...], b_ref[...])


def mm_ln_gdn(o_f, o_b, p, norm_w, x, gate, w, g, b, alpha, tm):
    B, N, D = x.shape
    tm = min(tm, N)
    row = lambda bi, i: (bi, i, 0)
    vec = lambda bi, i: (0, 0)
    return pl.pallas_call(
        functools.partial(_mm_ln_gdn_kernel, alpha=alpha),
        grid=(B, N // tm),
        in_specs=[pl.BlockSpec((1, tm, D), row),
                  pl.BlockSpec((1, tm, D), row),
                  pl.BlockSpec((1, tm, D), lambda bi, i: (bi, i, 3)),
                  pl.BlockSpec((1, GDN_HEAD_DIM), vec),
                  pl.BlockSpec((1, tm, D), row),
                  pl.BlockSpec((1, 1, D), lambda bi, i: (bi, 0, 0)),
                  pl.BlockSpec((D, D), vec),
                  pl.BlockSpec((1, D), vec),
                  pl.BlockSpec((1, D), vec)],
        out_specs=pl.BlockSpec((1, tm, D), row),
        out_shape=jax.ShapeDtypeStruct((B, N, D), F32),
        scratch_shapes=[pltpu.VMEM((D, D), BF16), pltpu.VMEM((tm, D), BF16)],
        compiler_params=_params(2, VMEM_MID),
        name="mm_ln_gdn",
    )(o_f, o_b, p, norm_w.reshape(1, GDN_HEAD_DIM), x, gate, w, g.reshape(1, D), b.reshape(1, D))


def _conv_kernel(prev_ref, main_ref, next_ref, w_ref, o_ref, ext_ref, *, tt, nt):
    i = pl.program_id(1)
    j = pl.program_id(2)
    ext_ref[0:CONV_HALO, :] = jnp.where(i > 0, prev_ref[0], 0.0)
    ext_ref[CONV_HALO:CONV_HALO + tt, :] = main_ref[0]
    ext_ref[CONV_HALO + tt:, :] = jnp.where(i < nt - 1, next_ref[0], 0.0)
    pad = GDN_CONV // 2
    acc = w_ref[0:1, :] * ext_ref[CONV_HALO - pad:CONV_HALO - pad + tt, :]
    for t in range(1, GDN_CONV):
        acc = acc + w_ref[t:t + 1, :] * ext_ref[CONV_HALO - pad + t:CONV_HALO - pad + t + tt, :]
    y = _silu(acc)
    qscale = jnp.where(j == 0, GDN_HEAD_DIM ** -0.5, 1.0)
    for h in range(GDN_HEADS):
        sl = slice(h * GDN_HEAD_DIM, (h + 1) * GDN_HEAD_DIM)
        seg = y[:, sl]
        inv = lax.rsqrt(jnp.sum(seg * seg, -1, keepdims=True) + 1e-6) * qscale
        o_ref[0, :, sl] = seg * jnp.where(j < 2, inv, 1.0)


def gdn_conv(p, conv_w, tt):
    B, T = p.shape[:2]
    D = conv_w.shape[1] // 3
    tt = min(tt, T)
    nt = T // tt
    hb = tt // CONV_HALO
    return pl.pallas_call(
        functools.partial(_conv_kernel, tt=tt, nt=nt),
        grid=(B, nt, 3),
        in_specs=[pl.BlockSpec((1, CONV_HALO, D), lambda b, i, j: (b, jnp.maximum(i * hb - 1, 0), j)),
                  pl.BlockSpec((1, tt, D), lambda b, i, j: (b, i, j)),
                  pl.BlockSpec((1, CONV_HALO, D), lambda b, i, j: (b, jnp.minimum((i + 1) * hb, T // CONV_HALO - 1), j)),
                  pl.BlockSpec((GDN_CONV, D), lambda b, i, j: (0, j))],
        out_specs=pl.BlockSpec((1, tt, D), lambda b, i, j: (b, i, j)),
        out_shape=jax.ShapeDtypeStruct((B, T, 3 * D), F32),
        scratch_shapes=[pltpu.VMEM((tt + 2 * CONV_HALO, D), F32)],
        compiler_params=_params(3, VMEM_MID),
        name="gdn_conv",
    )(p, p, p, conv_w)


def _gates_kernel(ab_ref, alog_ref, dtb_ref, o_ref):
    ab = ab_ref[0]
    xg = ab + dtb_ref[...]
    softplus = jnp.maximum(xg, 0.0) + jnp.log(1.0 + jnp.exp(-jnp.abs(xg)))
    g = -jnp.exp(alog_ref[...]) * softplus
    beta = jax.nn.sigmoid(ab)
    lane = lax.broadcasted_iota(I32, ab.shape, 1)
    o_ref[0] = jnp.where(lane < 2 * GDN_HEADS, g, beta)


def gdn_gates(ab, a_log, dt_bias, tt):
    B, T, W = ab.shape
    tt = min(tt, T)
    pad = lambda a: jnp.pad(a.reshape(1, -1).astype(F32), ((0, 0), (0, W - a.size)))
    return pl.pallas_call(
        _gates_kernel,
        grid=(B, T // tt),
        in_specs=[pl.BlockSpec((1, tt, W), lambda b, i: (b, i, 0)),
                  pl.BlockSpec((1, W), lambda b, i: (0, 0)),
                  pl.BlockSpec((1, W), lambda b, i: (0, 0))],
        out_specs=pl.BlockSpec((1, tt, W), lambda b, i: (b, i, 0)),
        out_shape=jax.ShapeDtypeStruct((B, T, W), F32),
        compiler_params=_params(2, VMEM_MID),
        name="gdn_gates",
    )(ab, pad(a_log), pad(dt_bias))


TRI_BASE = 16


def _bmm(a, b):
    return jnp.einsum("hij,hjk->hik", a.astype(BF16), b.astype(BF16), preferred_element_type=F32)


def _unit_triangular_inverse(A, eye):
    C = A.shape[-1]
    r = lax.broadcasted_iota(I32, (C, C), 0)
    c = lax.broadcasted_iota(I32, (C, C), 1)
    sh = int(math.log2(TRI_BASE))
    B = jnp.where((r >> sh) == (c >> sh), A, 0.0)
    T = eye - B
    P = B
    for _ in range(sh - 1):
        P = _bmm(P, P)
        T = T + _bmm(T, P)
    while sh < int(math.log2(C)):
        off = ((r >> (sh + 1)) == (c >> (sh + 1))) & ((r >> sh) != (c >> sh))
        T = T - _bmm(_bmm(T, jnp.where(off, A, 0.0)), T)
        sh += 1
    return T


def _gdn_stream_operands(d, q_ref, k_ref, v_ref, gb, Gc_all, Gr_all, incl, strict):
    C, H, dk = GDN_CHUNK, GDN_HEADS, GDN_HEAD_DIM
    last = C - 1 if d == 0 else 0
    cols = [d * H + h for h in range(H)]
    heads = lambda f: jnp.stack([f(h, cols[h]) for h in range(H)])
    Gc = heads(lambda h, col: Gc_all[:, col:col + 1])
    Gr = heads(lambda h, col: Gr_all[col:col + 1, :])
    decay = jnp.where(incl, jnp.exp(jnp.minimum(Gc - Gr, 0.0)), 0.0)
    return dict(
        Gc=Gc,
        Gtot=heads(lambda h, col: Gc_all[last:last + 1, col:col + 1]),
        beta=heads(lambda h, col: gb[:, 2 * H + col:2 * H + col + 1]),
        q=heads(lambda h, col: q_ref[0, :, h * dk:(h + 1) * dk]),
        k=heads(lambda h, col: k_ref[0, :, h * dk:(h + 1) * dk]),
        v=heads(lambda h, col: v_ref[0, :, h * dk:(h + 1) * dk]),
        decay=decay,
        decay_strict=jnp.where(strict, decay, 0.0))


def _gdn_chunk(ops, S, eye):
    C, dk = GDN_CHUNK, GDN_HEAD_DIM
    q, k, v, beta, Gc, Gtot = (ops[name] for name in ("q", "k", "v", "beta", "Gc", "Gtot"))
    kb = k * beta
    kq = jnp.einsum("hid,hjd->hij", jnp.concatenate([kb, q], axis=1).astype(BF16), k.astype(BF16),
                    preferred_element_type=F32)
    A = kq[:, :C] * ops["decay_strict"]
    attn = kq[:, C:] * ops["decay"]
    T = _unit_triangular_inverse(A, eye)
    eG = jnp.exp(Gc)
    wu = _bmm(T, jnp.concatenate([kb * eG, v * beta], axis=2))
    wq = _bmm(jnp.concatenate([wu[:, :, :dk], q * eG], axis=1), S)
    v_new = wu[:, :, dk:] - wq[:, :C]
    o = wq[:, C:] + _bmm(attn, v_new)
    k_tail = k * jnp.exp(Gtot - Gc)
    S_new = S * jnp.exp(Gtot) + jnp.einsum("hcd,hce->hde", k_tail.astype(BF16), v_new.astype(BF16),
                                           preferred_element_type=F32)
    return o, S_new


def _gdn_scan_kernel(qf_ref, kf_ref, vf_ref, gbf_ref, gtf_ref, qb_ref, kb_ref, vb_ref, gbb_ref, gtb_ref,
                     s0_ref, of_ref, ob_ref, sout_ref, S_ref, *, nsteps):
    i = pl.program_id(1)

    @pl.when(i == 0)
    def _():
        S_ref[...] = s0_ref[0]

    C = GDN_CHUNK
    r = lax.broadcasted_iota(I32, (C, C), 0)
    c = lax.broadcasted_iota(I32, (C, C), 1)
    eye = jnp.where(r == c, 1.0, 0.0).astype(F32)
    streams = ((qf_ref, kf_ref, vf_ref, gbf_ref, gtf_ref, of_ref, r >= c, r > c),
               (qb_ref, kb_ref, vb_ref, gbb_ref, gtb_ref, ob_ref, r <= c, r < c))
    per_stream = []
    for d, (q_ref, k_ref, v_ref, gb_ref, gt_ref, o_ref, incl, strict) in enumerate(streams):
        tri = jnp.where(incl, 1.0, 0.0).astype(F32)
        gb = gb_ref[0]
        Gc_all = jnp.dot(tri, gb, preferred_element_type=F32, precision=HIGHEST)
        Gr_all = lax.dot_general(gt_ref[0], tri, _NT, preferred_element_type=F32, precision=HIGHEST)
        per_stream.append(_gdn_stream_operands(d, q_ref, k_ref, v_ref, gb, Gc_all, Gr_all, incl, strict))
    ops = {name: jnp.concatenate([p[name] for p in per_stream], axis=0) for name in per_stream[0]}
    H, dk = GDN_HEADS, GDN_HEAD_DIM
    o, S_new = _gdn_chunk(ops, S_ref[...].reshape(2 * H, dk, dk), eye)
    S_ref[...] = S_new.reshape(2, H, dk, dk)
    for d, stream in enumerate(streams):
        o_ref = stream[5]
        for h in range(H):
            o_ref[0, :, h * dk:(h + 1) * dk] = o[d * H + h]

    @pl.when(i == nsteps - 1)
    def _():
        sout_ref[0] = S_ref[...]


def gdn_scan(qkv, gb, gbt, s0):
    B, T, D3 = qkv.shape
    D = D3 // 3
    C = GDN_CHUNK
    n = T // C
    W = gb.shape[2]
    R = gbt.shape[1]
    fwd = lambda j: (lambda b, i: (b, i, j))
    bwd = lambda j: (lambda b, i: (b, n - 1 - i, j))
    st = lambda b, i: (b, 0, 0, 0, 0)
    sspec = pl.BlockSpec((1, 2, GDN_HEADS, GDN_HEAD_DIM, GDN_HEAD_DIM), st)
    return pl.pallas_call(
        functools.partial(_gdn_scan_kernel, nsteps=n),
        grid=(B, n),
        in_specs=[pl.BlockSpec((1, C, D), fwd(0)), pl.BlockSpec((1, C, D), fwd(1)), pl.BlockSpec((1, C, D), fwd(2)),
                  pl.BlockSpec((1, C, W), fwd(0)), pl.BlockSpec((1, R, C), lambda b, i: (b, 0, i)),
                  pl.BlockSpec((1, C, D), bwd(0)), pl.BlockSpec((1, C, D), bwd(1)), pl.BlockSpec((1, C, D), bwd(2)),
                  pl.BlockSpec((1, C, W), bwd(0)), pl.BlockSpec((1, R, C), lambda b, i: (b, 0, n - 1 - i)),
                  sspec],
        out_specs=[pl.BlockSpec((1, C, D), fwd(0)), pl.BlockSpec((1, C, D), bwd(0)), sspec],
        out_shape=[jax.ShapeDtypeStruct((B, T, D), F32), jax.ShapeDtypeStruct((B, T, D), F32),
                   jax.ShapeDtypeStruct(s0.shape, F32)],
        scratch_shapes=[pltpu.VMEM((2, GDN_HEADS, GDN_HEAD_DIM, GDN_HEAD_DIM), F32)],
        compiler_params=_params(2, VMEM_MID),
        name="gdn_scan",
    )(qkv, qkv, qkv, gb, gbt, qkv, qkv, qkv, gb, gbt, s0)


def _router_kernel(x_ref, sc_ref, sh_ref, wr_ref, o_ref):
    h = x_ref[0] * (1.0 + sc_ref[0]) + sh_ref[0]
    w = wr_ref[...]
    w_hi = w.astype(BF16)
    w_lo = (w - w_hi.astype(F32)).astype(BF16)
    h_hi = h.astype(BF16)
    h_lo = (h - h_hi.astype(F32)).astype(BF16)
    nt = lambda a, b: lax.dot_general(a, b, _NT, preferred_element_type=F32)
    lg = nt(w_hi, h_hi) + (nt(w_hi, h_lo) + nt(w_lo, h_hi))
    e = jnp.exp(lg - jnp.max(lg, 0, keepdims=True))
    o_ref[0] = e / jnp.sum(e, 0, keepdims=True)


def router(x, sc, sh, w_router_t, tt):
    B, n, D = x.shape
    E = w_router_t.shape[0]
    tt = min(tt, n)
    return pl.pallas_call(
        _router_kernel,
        grid=(B, n // tt),
        in_specs=[pl.BlockSpec((1, tt, D), lambda b, i: (b, i, 0)),
                  pl.BlockSpec((1, 1, D), lambda b, i: (b, 0, 0)),
                  pl.BlockSpec((1, 1, D), lambda b, i: (b, 0, 0)),
                  pl.BlockSpec((E, D), lambda b, i: (0, 0))],
        out_specs=pl.BlockSpec((1, E, tt), lambda b, i: (b, 0, i)),
        out_shape=jax.ShapeDtypeStruct((B, E, n), F32),
        compiler_params=_params(2, VMEM_MID),
        name="router",
    )(x, sc, sh, w_router_t)


def _exclusive_cumsum_lanes(mask, n):
    r = lax.broadcasted_iota(I32, (LANES, LANES), 0)
    c = lax.broadcasted_iota(I32, (LANES, LANES), 1)
    upper = jnp.where(r < c, 1.0, 0.0).astype(BF16)
    carry = jnp.zeros((mask.shape[0], 1), F32)
    out = []
    for j in range(n // LANES):
        blk = mask[:, j * LANES:(j + 1) * LANES]
        out.append(jnp.dot(blk.astype(BF16), upper, preferred_element_type=F32) + carry)
        carry = carry + jnp.sum(blk, -1, keepdims=True)
    return jnp.concatenate(out, axis=1)


def _topk_kernel(aff_ref, idx_ref, gate_ref, hi_ref, lo_ref, pieces_ref, *, n, cap, n_hi, n_hi_pad):
    E = aff_ref.shape[1]
    aff = aff_ref[0]
    bits = pltpu.bitcast(aff, I32)
    count = lambda m: jnp.sum(jnp.where(m, 1.0, 0.0), -1, keepdims=True)
    t = jnp.zeros((E, 1), I32)
    for bit in range(30, -1, -1):
        cand = t | (1 << bit)
        t = jnp.where(count(bits >= cand) >= cap, cand, t)
    gt = bits > t
    eq = bits == t
    need = cap - count(gt)
    eq_rank = _exclusive_cumsum_lanes(jnp.where(eq, 1.0, 0.0), n)
    sel = gt | (eq & (eq_rank < need))
    pos = _exclusive_cumsum_lanes(jnp.where(sel, 1.0, 0.0), n).astype(I32)
    hi_ref[...] = jnp.where(sel, pos >> int(math.log2(TOPK_LO)), -1)
    lo_ref[...] = pos & (TOPK_LO - 1)
    g_hi = aff.astype(BF16).astype(F32)
    g_mid = (aff - g_hi).astype(BF16).astype(F32)
    pieces_ref[0 * E:1 * E, :] = g_hi
    pieces_ref[1 * E:2 * E, :] = g_mid
    pieces_ref[2 * E:3 * E, :] = aff - g_hi - g_mid
    tok = lax.broadcasted_iota(I32, (1, n), 1)
    tok_hi = (tok >> 6).astype(F32)
    tok_lo = (tok & 63).astype(F32)
    hi_iota = lax.broadcasted_iota(I32, (n_hi_pad, n), 0)
    lo_iota = lax.broadcasted_iota(I32, (TOPK_LO, n), 0)

    def per_expert(e, carry):
        a = jnp.where(hi_ref[pl.ds(e, 1), :] == hi_iota, 1.0, 0.0).astype(BF16)
        b = jnp.where(lo_ref[pl.ds(e, 1), :] == lo_iota, 1.0, 0.0)
        vals = [tok_hi, tok_lo] + [pieces_ref[pl.ds(k * E + e, 1), :] for k in range(3)]
        rhs = jnp.concatenate([(b * v).astype(BF16) for v in vals], axis=0)
        res = lax.dot_general(a, rhs, _NT, preferred_element_type=F32)
        part = lambda k: res[:n_hi, k * TOPK_LO:(k + 1) * TOPK_LO]
        idx_ref[0, pl.ds(e, 1)] = (part(0) * 64.0 + part(1)).astype(I32)[None]
        gate_ref[0, pl.ds(e, 1)] = (part(2) + part(3) + part(4))[None]
        return carry

    lax.fori_loop(0, E, per_expert, 0, unroll=TOPK_ROW_UNROLL)


def topk_select(aff, cap):
    B, E, n = aff.shape
    R = B * E
    n_hi = cap // TOPK_LO
    n_hi_pad = -(-n_hi // SUBLANES) * SUBLANES
    idx, gate = pl.pallas_call(
        functools.partial(_topk_kernel, n=n, cap=cap, n_hi=n_hi, n_hi_pad=n_hi_pad),
        grid=(1,),
        in_specs=[pl.BlockSpec((1, R, n), lambda b: (0, 0, 0))],
        out_specs=[pl.BlockSpec((1, R, n_hi, TOPK_LO), lambda b: (0, 0, 0, 0)),
                   pl.BlockSpec((1, R, n_hi, TOPK_LO), lambda b: (0, 0, 0, 0))],
        out_shape=[jax.ShapeDtypeStruct((1, R, n_hi, TOPK_LO), I32), jax.ShapeDtypeStruct((1, R, n_hi, TOPK_LO), F32)],
        scratch_shapes=[pltpu.VMEM((R, n), I32), pltpu.VMEM((R, n), I32), pltpu.VMEM((3 * R, n), F32)],
        compiler_params=_params(1, VMEM_MID),
        name="topk_select",
    )(aff.reshape(1, R, n))
    return idx.reshape(B, E, cap), gate.reshape(B, E, cap)


def _moe_kernel(idx_ref, gate_ref, x_ref, sc_ref, sh_ref, og_ref, lng_ref, lnb_ref, wg_ref, wu_ref, wd_ref, y_ref,
                xs32_ref, xs_ref, ye_ref, yp_ref, *, S, nfc, alpha):
    e = pl.program_id(0)
    f = pl.program_id(1)
    E = pl.num_programs(0)
    per_step = S // nfc // SUBLANES

    def gather8(expert, base):
        for j in range(SUBLANES):
            i = idx_ref[expert * S + base + j]
            xs32_ref[pl.ds(base + j, 1), :] = x_ref[0, pl.ds(i, 1), :]

    def scatter8(expert, base, scale):
        rows, news = [], []
        for j in range(SUBLANES):
            i = idx_ref[expert * S + base + j]
            g = gate_ref[expert * S + base + j] * scale
            rows.append(pl.ds(i, 1))
            news.append(y_ref[0, pl.ds(i, 1), :] + g * yp_ref[pl.ds(base + j, 1), :])
        for r, new in zip(rows, news):
            y_ref[0, r, :] = new

    def modulated_bf16():
        return (xs32_ref[...] * (1.0 + sc_ref[0]) + sh_ref[0]).astype(BF16)

    @pl.when((e == 0) & (f == 0))
    def _():
        y_ref[0] = jnp.zeros(y_ref.shape[1:], F32)
        yp_ref[...] = jnp.zeros(yp_ref.shape, F32)

        def body(s8, carry):
            gather8(0, pl.multiple_of(s8 * SUBLANES, SUBLANES))
            return carry

        lax.fori_loop(0, S // SUBLANES, body, 0)
        xs_ref[...] = modulated_bf16()

    prev = jnp.maximum(e - 1, 0)
    prev_scale = jnp.where(e > 0, 1.0, 0.0)
    nxt = jnp.minimum(e + 1, E - 1)
    def row_ops(part_index, n_parts):
        for s8 in range(part_index, per_step, n_parts):
            base = pl.multiple_of((f * per_step + s8) * SUBLANES, SUBLANES)
            scatter8(prev, base, prev_scale)
            gather8(nxt, base)

    xs = xs_ref[...]
    row_ops(0, 3)
    a = jnp.dot(xs, wg_ref[0], preferred_element_type=F32)
    row_ops(1, 3)
    u = jnp.dot(xs, wu_ref[0], preferred_element_type=F32)
    row_ops(2, 3)
    part = jnp.dot((_silu(a) * u).astype(BF16), wd_ref[0], preferred_element_type=F32)

    @pl.when(f == 0)
    def _():
        ye_ref[...] = part

    @pl.when(f > 0)
    def _():
        ye_ref[...] += part

    @pl.when(f == nfc - 1)
    def _():
        yp_ref[...] = ye_ref[...]
        xs_ref[...] = modulated_bf16()

    @pl.when((e == E - 1) & (f == nfc - 1))
    def _():
        def body(s8, carry):
            scatter8(e, pl.multiple_of(s8 * SUBLANES, SUBLANES), 1.0)
            return carry

        lax.fori_loop(0, S // SUBLANES, body, 0)

        def ln_body(t, carry):
            rows = pl.ds(pl.multiple_of(t * MOE_LN_ROWS, MOE_LN_ROWS), MOE_LN_ROWS)
            z = alpha * x_ref[0, rows, :] + og_ref[0] * y_ref[0, rows, :]
            y_ref[0, rows, :] = _layernorm_rows(z, lng_ref[...], lnb_ref[...])
            return carry

        lax.fori_loop(0, y_ref.shape[1] // MOE_LN_ROWS, ln_body, 0)


def moe_experts(idx, gate, x, g, sc, sh, out_gate, ln_g, ln_b, wg, wu, wd, e0, n_experts, fc, alpha):
    G, R, D = x.shape
    F = wg.shape[2]
    E = n_experts
    S = idx.shape[0] // E
    nfc = F // fc
    grp = lambda e, f, *_: (g, 0, 0)
    vec = lambda e, f, *_: (0, 0)
    return pl.pallas_call(
        functools.partial(_moe_kernel, S=S, nfc=nfc, alpha=alpha),
        grid_spec=pltpu.PrefetchScalarGridSpec(
            num_scalar_prefetch=2,
            grid=(E, nfc),
            in_specs=[pl.BlockSpec((1, R, D), grp, pipeline_mode=pl.Buffered(1)),
                      pl.BlockSpec((1, 1, D), grp),
                      pl.BlockSpec((1, 1, D), grp),
                      pl.BlockSpec((1, 1, D), grp),
                      pl.BlockSpec((1, D), vec),
                      pl.BlockSpec((1, D), vec),
                      pl.BlockSpec((1, D, fc), lambda e, f, *_: (e0 + e, 0, f)),
                      pl.BlockSpec((1, D, fc), lambda e, f, *_: (e0 + e, 0, f)),
                      pl.BlockSpec((1, fc, D), lambda e, f, *_: (e0 + e, f, 0))],
            out_specs=pl.BlockSpec((1, R, D), grp, pipeline_mode=pl.Buffered(1)),
            scratch_shapes=[pltpu.VMEM((S, D), F32), pltpu.VMEM((S, D), BF16), pltpu.VMEM((S, D), F32),
                            pltpu.VMEM((S, D), F32)]),
        out_shape=jax.ShapeDtypeStruct((G, R, D), F32),
        input_output_aliases={2: 0},
        compiler_params=_params(2, VMEM_BIG),
        name="moe_experts",
    )(idx, gate, x, sc, sh, out_gate, ln_g.reshape(1, D), ln_b.reshape(1, D), wg, wu, wd)


def ec_moe_residual(x, sc, sh, out_gate, ln_g, ln_b, w_router_t, wg, wu, wd, e0, alpha, flatten_groups):
    B, n, D = x.shape
    E = w_router_t.shape[0]
    cap = EC_CAPACITY_FACTOR * n // E
    aff = router(x, sc, sh, w_router_t, 1024)
    idx, gate = topk_select(aff, cap)
    fc = min(MOE_HIDDEN_CHUNK, wg.shape[2])
    if flatten_groups:
        idx = idx + (jnp.arange(B, dtype=I32) * n)[:, None, None]
        idx = idx.transpose(1, 0, 2).reshape(-1)
        gate = gate.transpose(1, 0, 2).reshape(-1)
        x = moe_experts(idx, gate, x.reshape(1, B * n, D), 0, sc[:1], sh[:1], out_gate[:1], ln_g, ln_b,
                        wg, wu, wd, e0, E, fc, alpha)
        return x.reshape(B, n, D)
    for b in range(B):
        x = moe_experts(idx[b].reshape(-1), gate[b].reshape(-1), x, b, sc, sh, out_gate, ln_g, ln_b,
                        wg, wu, wd, e0, E, fc, alpha)
    return x


def kernel(x, c, ctx, c_ctx, ada_w, ada_b, ln_g, ln_b, na_w_qkv, na_w_o, na_rpb, gdn_w_in, gdn_conv_w,
           gdn_a_log, gdn_dt_bias, gdn_norm_w, gdn_w_o, moe_w_router, moe_w_gate, moe_w_up, moe_w_down):
    B, N, D = x.shape
    L = ctx.shape[1]
    depth = ada_w.shape[0]
    alpha = (2.0 * depth) ** 0.25
    xc = ctx

    rows = -(-(B + 1) // SUBLANES) * SUBLANES
    cs = jnp.zeros((rows, D), F32).at[:B].set(c).at[B].set(c_ctx)
    mods = ada_modulation(cs, ada_w, ada_b)

    E = moe_w_gate.shape[1]
    stack = lambda w: w.astype(BF16).reshape((depth * E,) + w.shape[2:])
    wg_all, wu_all, wd_all = stack(moe_w_gate), stack(moe_w_up), stack(moe_w_down)

    for l in range(depth):
        last = l == depth - 1
        i = l // 2
        mod = [mods[l, :B, j * D:(j + 1) * D].reshape(B, 1, D) for j in range(6)]
        modc = [jnp.broadcast_to(mods[l, B, j * D:(j + 1) * D].reshape(1, 1, D), (B, 1, D)) for j in range(6)]
        g1, b1, g2, b2 = ln_g[l, 0], ln_b[l, 0], ln_g[l, 1], ln_b[l, 1]
        if l % 2 == 0:
            w_qkv = na_w_qkv[i].astype(BF16)
            qkv = mm_mod(x, mod[1], mod[0], w_qkv, BF16, 512)
            qkvc = mm_mod(xc, modc[1], modc[0], w_qkv, BF16, 512)
            o = natten(qkv, qkvc, _natten_bias_table(na_rpb[i]))
            x = mm_ln(o, x, mod[2], na_w_o[i], g1, b1, alpha, 512)
            if not last:
                xc = mm_ln(ctx_attention(qkvc), xc, modc[2], na_w_o[i], g1, b1, alpha, 512)
        else:
            w_main = gdn_w_in[i][:, :4 * D].astype(BF16)
            w_ab = jnp.pad(gdn_w_in[i][:, 4 * D:], ((0, 0), (0, LANES - 4 * GDN_HEADS))).astype(BF16)

            def project(u, m):
                p = mm_mod(u, m[1], m[0], w_main, F32, 512)
                ab = mm_mod(u, m[1], m[0], w_ab, F32, 1024)
                gb = gdn_gates(ab, gdn_a_log[i], gdn_dt_bias[i], 1024)
                gbt = jnp.swapaxes(gb[:, :, :4 * GDN_HEADS], 1, 2)
                return p, gdn_conv(p, gdn_conv_w[i], 512), gb, gbt

            p, qkv, gb, gbt = project(x, mod)
            pc, qkvc, gbc, gbtc = project(xc, modc)
            s0 = jnp.zeros((B, 2, GDN_HEADS, GDN_HEAD_DIM, GDN_HEAD_DIM), F32)
            oc_f, oc_b, s_ctx = gdn_scan(qkvc, gbc, gbtc, s0)
            o_f, o_b, _ = gdn_scan(qkv, gb, gbt, s_ctx)
            x = mm_ln_gdn(o_f, o_b, p, gdn_norm_w[i], x, mod[2], gdn_w_o[i], g1, b1, alpha, 512)
            if not last:
                xc = mm_ln_gdn(oc_f, oc_b, pc, gdn_norm_w[i], xc, modc[2], gdn_w_o[i], g1, b1, alpha, 512)
        wr_t = moe_w_router[l].T
        x = ec_moe_residual(x, mod[4], mod[3], mod[5], g2, b2, wr_t, wg_all, wu_all, wd_all, l * E, alpha, False)
        if not last:
            xc = ec_moe_residual(xc, modc[4], modc[3], modc[5], g2, b2, wr_t, wg_all, wu_all, wd_all, l * E, alpha,
                                 True)
    return x
```

```python
import functools
import math

import numpy as np
import jax
import jax.numpy as jnp
from jax import lax
from jax.experimental import pallas as pl
from jax.experimental.pallas import tpu as pltpu

F32 = jnp.float32
BF16 = jnp.bfloat16
I32 = jnp.int32
HIGHEST = lax.Precision.HIGHEST

GRID_W = 64
NA_HEADS = 16
NA_HEAD_DIM = 64
NA_KR = 8
NA_KC = 16
GDN_HEADS = 8
GDN_HEAD_DIM = 128
GDN_CONV = 5
N_EXPERTS = 16
EC_CAPACITY_FACTOR = 2
LN_EPS = 1e-6
NEG_INF = -1e30

GDN_CHUNK = 128
CONV_HALO = 8
SUBLANES = 8
LANES = 128
MOE_HIDDEN_CHUNK = 1024
MOE_LN_ROWS = 256
TOPK_LO = 16
TOPK_ROW_UNROLL = 8
V7X_VMEM_BYTES = 64 * 1024 * 1024
VMEM_BIG = 56 * 1024 * 1024
VMEM_MID = 40 * 1024 * 1024

_NT = (((1,), (1,)), ((), ()))
_TN = (((0,), (0,)), ((), ()))


def _params(n_axes, vmem):
    return pltpu.CompilerParams(dimension_semantics=("arbitrary",) * n_axes, vmem_limit_bytes=vmem)


def _silu(x):
    return x * jax.nn.sigmoid(x)


def _layernorm_rows(z, g, b):
    mu = jnp.mean(z, -1, keepdims=True)
    zc = z - mu
    var = jnp.mean(zc * zc, -1, keepdims=True)
    return zc * lax.rsqrt(var + LN_EPS) * g + b


def _ada_kernel(cs_ref, w_ref, b_ref, o_ref):
    s = _silu(cs_ref[...])
    o_ref[0] = jnp.dot(s, w_ref[0], preferred_element_type=F32, precision=HIGHEST) + b_ref[0]


def ada_modulation(cs, ada_w, ada_b):
    depth, D, D6 = ada_w.shape
    R = cs.shape[0]
    tn = D6 // 4
    return pl.pallas_call(
        _ada_kernel,
        grid=(depth, D6 // tn),
        in_specs=[pl.BlockSpec((R, D), lambda l, j: (0, 0)),
                  pl.BlockSpec((1, D, tn), lambda l, j: (l, 0, j)),
                  pl.BlockSpec((1, 1, tn), lambda l, j: (l, 0, j))],
        out_specs=pl.BlockSpec((1, R, tn), lambda l, j: (l, 0, j)),
        out_shape=jax.ShapeDtypeStruct((depth, R, D6), F32),
        compiler_params=_params(2, VMEM_MID),
        name="ada_modulation",
    )(cs, ada_w, ada_b.reshape(depth, 1, D6))


def _mm_mod_kernel(x_ref, sc_ref, sh_ref, w_ref, o_ref):
    h = x_ref[0] * (1.0 + sc_ref[0]) + sh_ref[0]
    o_ref[0] = jnp.dot(h.astype(BF16), w_ref[...], preferred_element_type=F32).astype(o_ref.dtype)


def mm_mod(x, sc, sh, w, out_dtype, tm):
    B, N, D = x.shape
    NO = w.shape[1]
    tm = min(tm, N)
    return pl.pallas_call(
        _mm_mod_kernel,
        grid=(B, N // tm),
        in_specs=[pl.BlockSpec((1, tm, D), lambda b, i: (b, i, 0)),
                  pl.BlockSpec((1, 1, D), lambda b, i: (b, 0, 0)),
                  pl.BlockSpec((1, 1, D), lambda b, i: (b, 0, 0)),
                  pl.BlockSpec((D, NO), lambda b, i: (0, 0), pipeline_mode=pl.Buffered(1))],
        out_specs=pl.BlockSpec((1, tm, NO), lambda b, i: (b, i, 0)),
        out_shape=jax.ShapeDtypeStruct((B, N, NO), out_dtype),
        compiler_params=_params(2, VMEM_MID),
        name="mm_mod",
    )(x, sc, sh, w)


def _natten_bias_table(rpb):
    H = rpb.shape[0]
    qc = np.arange(GRID_W)
    kc = np.arange(GRID_W)
    col_start = np.clip(qc - NA_KC // 2, 0, GRID_W - NA_KC)
    valid = (kc[None, :] >= col_start[:, None]) & (kc[None, :] < col_start[:, None] + NA_KC)
    dc = np.clip(kc[None, :] - qc[:, None], -(NA_KC - 1), NA_KC - 1) + NA_KC - 1
    tab = rpb.astype(F32)[:, :, dc] + jnp.where(jnp.asarray(valid), 0.0, NEG_INF).astype(F32)[None, None]
    variants = [tab[:, d0:d0 + NA_KR].transpose(0, 2, 1, 3).reshape(H, GRID_W, NA_KR * GRID_W)
                for d0 in range(NA_KR)]
    return jnp.stack(variants)


def _softmax_pv(s_list, v_list):
    m = functools.reduce(jnp.maximum, [jnp.max(s, -1, keepdims=True) for s in s_list])
    ps = [jnp.exp(s - m) for s in s_list]
    den = functools.reduce(lambda a, b: a + b, [jnp.sum(p, -1, keepdims=True) for p in ps])
    o = functools.reduce(lambda a, b: a + b,
                         [jnp.dot(p.astype(BF16), v, preferred_element_type=F32) for p, v in zip(ps, v_list)])
    return o / den


def _natten_kernel(q_ref, k_ref, v_ref, kc_ref, vc_ref, bias_ref, o_ref, *, rows, scale):
    r = pl.program_id(1)
    rs = jnp.clip(r - NA_KR // 2, 0, rows - NA_KR)
    start = pl.multiple_of(rs * GRID_W, GRID_W)
    win = pl.ds(start, NA_KR * GRID_W)
    lane = lax.broadcasted_iota(I32, (GRID_W, LANES), 1)
    per_tile = LANES // NA_HEAD_DIM
    masks = [(lane >= half * NA_HEAD_DIM) & (lane < (half + 1) * NA_HEAD_DIM) for half in range(per_tile)]
    tiles = [slice(pair * LANES, (pair + 1) * LANES) for pair in range(NA_HEADS // per_tile)]
    scores = []
    for pair, sl in enumerate(tiles):
        q2 = q_ref[0, :, sl] * scale
        for half in range(per_tile):
            qh = jnp.where(masks[half], q2, jnp.zeros_like(q2))
            s_w = lax.dot_general(qh, k_ref[0, win, sl], _NT, preferred_element_type=F32)
            s_c = lax.dot_general(qh, kc_ref[0, :, sl], _NT, preferred_element_type=F32)
            scores.append((s_w + bias_ref[0, pair * per_tile + half], s_c))
    maxes = [jnp.maximum(jnp.max(s_w, -1, keepdims=True), jnp.max(s_c, -1, keepdims=True)) for s_w, s_c in scores]
    probs = [(jnp.exp(s_w - m), jnp.exp(s_c - m)) for (s_w, s_c), m in zip(scores, maxes)]
    dens = [jnp.sum(p_w, -1, keepdims=True) + jnp.sum(p_c, -1, keepdims=True) for p_w, p_c in probs]
    for pair, sl in enumerate(tiles):
        o = None
        for half in range(per_tile):
            h = pair * per_tile + half
            p_w, p_c = probs[h]
            pv = (jnp.dot(p_w.astype(BF16), v_ref[0, win, sl], preferred_element_type=F32)
                  + jnp.dot(p_c.astype(BF16), vc_ref[0, :, sl], preferred_element_type=F32)) / dens[h]
            o = pv if o is None else jnp.where(masks[half], pv, o)
        o_ref[0, :, sl] = o.astype(o_ref.dtype)


def natten(qkv, qkvc, bias_tab):
    B, N, D3 = qkv.shape
    D = D3 // 3
    L = qkvc.shape[1]
    rows = N // GRID_W
    assert rows >= NA_KR and N % GRID_W == 0
    scale = NA_HEAD_DIM ** -0.5
    assert math.frexp(scale)[0] == 0.5, "q is pre-scaled in bf16: the scale must be a power of two"

    def bias_index(b, r):
        rs = jnp.clip(r - NA_KR // 2, 0, rows - NA_KR)
        return (rs - r + NA_KR - 1, 0, 0, 0)

    return pl.pallas_call(
        functools.partial(_natten_kernel, rows=rows, scale=scale),
        grid=(B, rows),
        in_specs=[pl.BlockSpec((1, GRID_W, D), lambda b, r: (b, r, 0)),
                  pl.BlockSpec((1, N, D), lambda b, r: (b, 0, 1)),
                  pl.BlockSpec((1, N, D), lambda b, r: (b, 0, 2)),
                  pl.BlockSpec((1, L, D), lambda b, r: (b, 0, 1)),
                  pl.BlockSpec((1, L, D), lambda b, r: (b, 0, 2)),
                  pl.BlockSpec((1, NA_HEADS, GRID_W, NA_KR * GRID_W), bias_index)],
        out_specs=pl.BlockSpec((1, GRID_W, D), lambda b, r: (b, r, 0)),
        out_shape=jax.ShapeDtypeStruct((B, N, D), BF16),
        compiler_params=_params(2, VMEM_BIG),
        name="natten",
    )(qkv, qkv, qkv, qkvc, qkvc, bias_tab)


def _ctx_attn_kernel(q_ref, k_ref, v_ref, o_ref, *, scale):
    for h in range(NA_HEADS):
        sl = slice(h * NA_HEAD_DIM, (h + 1) * NA_HEAD_DIM)
        s = lax.dot_general(q_ref[0, :, sl], k_ref[0, :, sl], _NT, preferred_element_type=F32) * scale
        o_ref[0, :, sl] = _softmax_pv([s], [v_ref[0, :, sl]]).astype(o_ref.dtype)


def ctx_attention(qkvc):
    B, L, D3 = qkvc.shape
    D = D3 // 3
    return pl.pallas_call(
        functools.partial(_ctx_attn_kernel, scale=NA_HEAD_DIM ** -0.5),
        grid=(B,),
        in_specs=[pl.BlockSpec((1, L, D), lambda b: (b, 0, 0)),
                  pl.BlockSpec((1, L, D), lambda b: (b, 0, 1)),
                  pl.BlockSpec((1, L, D), lambda b: (b, 0, 2))],
        out_specs=pl.BlockSpec((1, L, D), lambda b: (b, 0, 0)),
        out_shape=jax.ShapeDtypeStruct((B, L, D), BF16),
        compiler_params=_params(1, VMEM_MID),
        name="ctx_attention",
    )(qkvc, qkvc, qkvc)


def _is_first_step():
    return (pl.program_id(0) == 0) & (pl.program_id(1) == 0)


def _mm_ln_kernel(a_ref, x_ref, gate_ref, w_ref, g_ref, b_ref, o_ref, wb_ref, *, alpha):
    @pl.when(_is_first_step())
    def _():
        wb_ref[...] = w_ref[...].astype(BF16)

    y = jnp.dot(a_ref[0], wb_ref[...], preferred_element_type=F32)
    o_ref[0] = _layernorm_rows(alpha * x_ref[0] + gate_ref[0] * y, g_ref[...], b_ref[...])


def mm_ln(a, x, gate, w, g, b, alpha, tm):
    B, N, D = x.shape
    tm = min(tm, N)
    row = lambda bi, i: (bi, i, 0)
    vec = lambda bi, i: (0, 0)
    return pl.pallas_call(
        functools.partial(_mm_ln_kernel, alpha=alpha),
        grid=(B, N // tm),
        in_specs=[pl.BlockSpec((1, tm, D), row),
                  pl.BlockSpec((1, tm, D), row),
                  pl.BlockSpec((1, 1, D), lambda bi, i: (bi, 0, 0)),
                  pl.BlockSpec((D, D), vec),
                  pl.BlockSpec((1, D), vec),
                  pl.BlockSpec((1, D), vec)],
        out_specs=pl.BlockSpec((1, tm, D), row),
        out_shape=jax.ShapeDtypeStruct((B, N, D), F32),
        scratch_shapes=[pltpu.VMEM((D, D), BF16)],
        compiler_params=_params(2, VMEM_MID),
        name="mm_ln",
    )(a, x, gate, w, g.reshape(1, D), b.reshape(1, D))


def _mm_ln_gdn_kernel(of_ref, ob_ref, z_ref, nw_ref, x_ref, gate_ref, w_ref, g_ref, b_ref, o_ref,
                      wb_ref, a_ref, *, alpha):
    @pl.when(_is_first_step())
    def _():
        wb_ref[...] = w_ref[...].astype(BF16)

    for h in range(GDN_HEADS):
        sl = slice(h * GDN_HEAD_DIM, (h + 1) * GDN_HEAD_DIM)
        o = of_ref[0, :, sl] + ob_ref[0, :, sl]
        y = o * lax.rsqrt(jnp.mean(o * o, -1, keepdims=True) + LN_EPS) * nw_ref[...]
        a_ref[:, sl] = (y * _silu(z_ref[0, :, sl])).astype(BF16)
    y = jnp.dot(a_ref[...], wb_ref[...], preferred_element_type=F32)
    o_ref[0] = _layernorm_rows(alpha * x_ref[0] + gate_ref[0] * y, g_ref[...], b_ref[...])


def mm_ln_gdn(o_f, o_b, p, norm_w, x, gate, w, g, b, alpha, tm):
    B, N, D = x.shape
    tm = min(tm, N)
    row = lambda bi, i: (bi, i, 0)
    vec = lambda bi, i: (0, 0)
    return pl.pallas_call(
        functools.partial(_mm_ln_gdn_kernel, alpha=alpha),
        grid=(B, N // tm),
        in_specs=[pl.BlockSpec((1, tm, D), row),
                  pl.BlockSpec((1, tm, D), row),
                  pl.BlockSpec((1, tm, D), lambda bi, i: (bi, i, 3)),
                  pl.BlockSpec((1, GDN_HEAD_DIM), vec),
                  pl.BlockSpec((1, tm, D), row),
                  pl.BlockSpec((1, 1, D), lambda bi, i: (bi, 0, 0)),
                  pl.BlockSpec((D, D), vec),
                  pl.BlockSpec((1, D), vec),
                  pl.BlockSpec((1, D), vec)],
        out_specs=pl.BlockSpec((1, tm, D), row),
        out_shape=jax.ShapeDtypeStruct((B, N, D), F32),
        scratch_shapes=[pltpu.VMEM((D, D), BF16), pltpu.VMEM((tm, D), BF16)],
        compiler_params=_params(2, VMEM_MID),
        name="mm_ln_gdn",
    )(o_f, o_b, p, norm_w.reshape(1, GDN_HEAD_DIM), x, gate, w, g.reshape(1, D), b.reshape(1, D))


def _conv_kernel(prev_ref, main_ref, next_ref, w_ref, o_ref, ext_ref, *, tt, nt):
    i = pl.program_id(1)
    j = pl.program_id(2)
    ext_ref[0:CONV_HALO, :] = jnp.where(i > 0, prev_ref[0], 0.0)
    ext_ref[CONV_HALO:CONV_HALO + tt, :] = main_ref[0]
    ext_ref[CONV_HALO + tt:, :] = jnp.where(i < nt - 1, next_ref[0], 0.0)
    pad = GDN_CONV // 2
    ext = ext_ref[...]
    rows = ext.shape[0]
    acc = None
    for t in range(GDN_CONV):
        shifted = ext if t == pad else pltpu.roll(ext, (pad - t) % rows, 0)
        term = w_ref[t:t + 1, :] * shifted[CONV_HALO:CONV_HALO + tt, :]
        acc = term if acc is None else acc + term
    y = _silu(acc)
    qscale = jnp.where(j == 0, GDN_HEAD_DIM ** -0.5, 1.0)
    for h in range(GDN_HEADS):
        sl = slice(h * GDN_HEAD_DIM, (h + 1) * GDN_HEAD_DIM)
        seg = y[:, sl]
        inv = lax.rsqrt(jnp.sum(seg * seg, -1, keepdims=True) + 1e-6) * qscale
        o_ref[0, :, sl] = seg * jnp.where(j < 2, inv, 1.0)


def gdn_conv(p, conv_w, tt):
    B, T = p.shape[:2]
    D = conv_w.shape[1] // 3
    tt = min(tt, T)
    nt = T // tt
    hb = tt // CONV_HALO
    return pl.pallas_call(
        functools.partial(_conv_kernel, tt=tt, nt=nt),
        grid=(B, nt, 3),
        in_specs=[pl.BlockSpec((1, CONV_HALO, D), lambda b, i, j: (b, jnp.maximum(i * hb - 1, 0), j)),
                  pl.BlockSpec((1, tt, D), lambda b, i, j: (b, i, j)),
                  pl.BlockSpec((1, CONV_HALO, D), lambda b, i, j: (b, jnp.minimum((i + 1) * hb, T // CONV_HALO - 1), j)),
                  pl.BlockSpec((GDN_CONV, D), lambda b, i, j: (0, j))],
        out_specs=pl.BlockSpec((1, tt, D), lambda b, i, j: (b, i, j)),
        out_shape=jax.ShapeDtypeStruct((B, T, 3 * D), F32),
        scratch_shapes=[pltpu.VMEM((tt + 2 * CONV_HALO, D), F32)],
        compiler_params=_params(3, VMEM_MID),
        name="gdn_conv",
    )(p, p, p, conv_w)


def _gates_kernel(ab_ref, alog_ref, dtb_ref, o_ref):
    ab = ab_ref[0]
    xg = ab + dtb_ref[...]
    softplus = jnp.maximum(xg, 0.0) + jnp.log(1.0 + jnp.exp(-jnp.abs(xg)))
    g = -jnp.exp(alog_ref[...]) * softplus
    beta = jax.nn.sigmoid(ab)
    lane = lax.broadcasted_iota(I32, ab.shape, 1)
    o_ref[0] = jnp.where(lane < 2 * GDN_HEADS, g, beta)


def gdn_gates(ab, a_log, dt_bias, tt):
    B, T, W = ab.shape
    tt = min(tt, T)
    pad = lambda a: jnp.pad(a.reshape(1, -1).astype(F32), ((0, 0), (0, W - a.size)))
    return pl.pallas_call(
        _gates_kernel,
        grid=(B, T // tt),
        in_specs=[pl.BlockSpec((1, tt, W), lambda b, i: (b, i, 0)),
                  pl.BlockSpec((1, W), lambda b, i: (0, 0)),
                  pl.BlockSpec((1, W), lambda b, i: (0, 0))],
        out_specs=pl.BlockSpec((1, tt, W), lambda b, i: (b, i, 0)),
        out_shape=jax.ShapeDtypeStruct((B, T, W), F32),
        compiler_params=_params(2, VMEM_MID),
        name="gdn_gates",
    )(ab, pad(a_log), pad(dt_bias))


TRI_BASE = 16


def _bmm(a, b):
    return jnp.einsum("hij,hjk->hik", a.astype(BF16), b.astype(BF16), preferred_element_type=F32)


def _unit_triangular_inverse(A, eye):
    C = A.shape[-1]
    r = lax.broadcasted_iota(I32, (C, C), 0)
    c = lax.broadcasted_iota(I32, (C, C), 1)
    sh = int(math.log2(TRI_BASE))
    B = jnp.where((r >> sh) == (c >> sh), A, 0.0)
    T = eye - B
    P = B
    for _ in range(sh - 1):
        P = _bmm(P, P)
        T = T + _bmm(T, P)
    while sh < int(math.log2(C)):
        off = ((r >> (sh + 1)) == (c >> (sh + 1))) & ((r >> sh) != (c >> sh))
        T = T - _bmm(_bmm(T, jnp.where(off, A, 0.0)), T)
        sh += 1
    return T


def _gdn_stream_operands(d, q_ref, k_ref, v_ref, gb, Gc_all, Gr_all, incl, strict):
    C, H, dk = GDN_CHUNK, GDN_HEADS, GDN_HEAD_DIM
    last = C - 1 if d == 0 else 0
    cols = [d * H + h for h in range(H)]
    heads = lambda f: jnp.stack([f(h, cols[h]) for h in range(H)])
    Gc = heads(lambda h, col: Gc_all[:, col:col + 1])
    Gr = heads(lambda h, col: Gr_all[col:col + 1, :])
    decay = jnp.where(incl, jnp.exp(jnp.minimum(Gc - Gr, 0.0)), 0.0)
    return dict(
        Gc=Gc,
        Gtot=heads(lambda h, col: Gc_all[last:last + 1, col:col + 1]),
        beta=heads(lambda h, col: gb[:, 2 * H + col:2 * H + col + 1]),
        q=heads(lambda h, col: q_ref[0, :, h * dk:(h + 1) * dk]),
        k=heads(lambda h, col: k_ref[0, :, h * dk:(h + 1) * dk]),
        v=heads(lambda h, col: v_ref[0, :, h * dk:(h + 1) * dk]),
        decay=decay,
        decay_strict=jnp.where(strict, decay, 0.0))


def _gdn_chunk(ops, S, eye):
    C, dk = GDN_CHUNK, GDN_HEAD_DIM
    q, k, v, beta, Gc, Gtot = (ops[name] for name in ("q", "k", "v", "beta", "Gc", "Gtot"))
    kb = k * beta
    kq = jnp.einsum("hid,hjd->hij", jnp.concatenate([kb, q], axis=1).astype(BF16), k.astype(BF16),
                    preferred_element_type=F32)
    A = kq[:, :C] * ops["decay_strict"]
    attn = kq[:, C:] * ops["decay"]
    T = _unit_triangular_inverse(A, eye)
    eG = jnp.exp(Gc)
    wu = _bmm(T, jnp.concatenate([kb * eG, v * beta], axis=2))
    wq = _bmm(jnp.concatenate([wu[:, :, :dk], q * eG], axis=1), S)
    v_new = wu[:, :, dk:] - wq[:, :C]
    o = wq[:, C:] + _bmm(attn, v_new)
    k_tail = k * jnp.exp(Gtot - Gc)
    S_new = S * jnp.exp(Gtot) + jnp.einsum("hcd,hce->hde", k_tail.astype(BF16), v_new.astype(BF16),
                                           preferred_element_type=F32)
    return o, S_new


def _gdn_scan_kernel(qf_ref, kf_ref, vf_ref, gbf_ref, gtf_ref, qb_ref, kb_ref, vb_ref, gbb_ref, gtb_ref,
                     s0_ref, of_ref, ob_ref, sout_ref, S_ref, *, nsteps):
    i = pl.program_id(1)

    @pl.when(i == 0)
    def _():
        S_ref[...] = s0_ref[0]

    C = GDN_CHUNK
    r = lax.broadcasted_iota(I32, (C, C), 0)
    c = lax.broadcasted_iota(I32, (C, C), 1)
    eye = jnp.where(r == c, 1.0, 0.0).astype(F32)
    streams = ((qf_ref, kf_ref, vf_ref, gbf_ref, gtf_ref, of_ref, r >= c, r > c),
               (qb_ref, kb_ref, vb_ref, gbb_ref, gtb_ref, ob_ref, r <= c, r < c))
    per_stream = []
    for d, (q_ref, k_ref, v_ref, gb_ref, gt_ref, o_ref, incl, strict) in enumerate(streams):
        tri = jnp.where(incl, 1.0, 0.0).astype(F32)
        gb = gb_ref[0]
        Gc_all = jnp.dot(tri, gb, preferred_element_type=F32, precision=HIGHEST)
        Gr_all = lax.dot_general(gt_ref[0], tri, _NT, preferred_element_type=F32, precision=HIGHEST)
        per_stream.append(_gdn_stream_operands(d, q_ref, k_ref, v_ref, gb, Gc_all, Gr_all, incl, strict))
    ops = {name: jnp.concatenate([p[name] for p in per_stream], axis=0) for name in per_stream[0]}
    H, dk = GDN_HEADS, GDN_HEAD_DIM
    o, S_new = _gdn_chunk(ops, S_ref[...].reshape(2 * H, dk, dk), eye)
    S_ref[...] = S_new.reshape(2, H, dk, dk)
    for d, stream in enumerate(streams):
        o_ref = stream[5]
        for h in range(H):
            o_ref[0, :, h * dk:(h + 1) * dk] = o[d * H + h]

    @pl.when(i == nsteps - 1)
    def _():
        sout_ref[0] = S_ref[...]


def gdn_scan(qkv, gb, gbt, s0):
    B, T, D3 = qkv.shape
    D = D3 // 3
    C = GDN_CHUNK
    n = T // C
    W = gb.shape[2]
    R = gbt.shape[1]
    fwd = lambda j: (lambda b, i: (b, i, j))
    bwd = lambda j: (lambda b, i: (b, n - 1 - i, j))
    st = lambda b, i: (b, 0, 0, 0, 0)
    sspec = pl.BlockSpec((1, 2, GDN_HEADS, GDN_HEAD_DIM, GDN_HEAD_DIM), st)
    return pl.pallas_call(
        functools.partial(_gdn_scan_kernel, nsteps=n),
        grid=(B, n),
        in_specs=[pl.BlockSpec((1, C, D), fwd(0)), pl.BlockSpec((1, C, D), fwd(1)), pl.BlockSpec((1, C, D), fwd(2)),
                  pl.BlockSpec((1, C, W), fwd(0)), pl.BlockSpec((1, R, C), lambda b, i: (b, 0, i)),
                  pl.BlockSpec((1, C, D), bwd(0)), pl.BlockSpec((1, C, D), bwd(1)), pl.BlockSpec((1, C, D), bwd(2)),
                  pl.BlockSpec((1, C, W), bwd(0)), pl.BlockSpec((1, R, C), lambda b, i: (b, 0, n - 1 - i)),
                  sspec],
        out_specs=[pl.BlockSpec((1, C, D), fwd(0)), pl.BlockSpec((1, C, D), bwd(0)), sspec],
        out_shape=[jax.ShapeDtypeStruct((B, T, D), F32), jax.ShapeDtypeStruct((B, T, D), F32),
                   jax.ShapeDtypeStruct(s0.shape, F32)],
        scratch_shapes=[pltpu.VMEM((2, GDN_HEADS, GDN_HEAD_DIM, GDN_HEAD_DIM), F32)],
        compiler_params=_params(2, VMEM_MID),
        name="gdn_scan",
    )(qkv, qkv, qkv, gb, gbt, qkv, qkv, qkv, gb, gbt, s0)


def _router_kernel(x_ref, sc_ref, sh_ref, wr_ref, o_ref):
    h = x_ref[0] * (1.0 + sc_ref[0]) + sh_ref[0]
    w = wr_ref[...]
    w_hi = w.astype(BF16)
    w_lo = (w - w_hi.astype(F32)).astype(BF16)
    h_hi = h.astype(BF16)
    h_lo = (h - h_hi.astype(F32)).astype(BF16)
    nt = lambda a, b: lax.dot_general(a, b, _NT, preferred_element_type=F32)
    lg = nt(w_hi, h_hi) + (nt(w_hi, h_lo) + nt(w_lo, h_hi))
    e = jnp.exp(lg - jnp.max(lg, 0, keepdims=True))
    o_ref[0] = e / jnp.sum(e, 0, keepdims=True)


def router(x, sc, sh, w_router_t, tt):
    B, n, D = x.shape
    E = w_router_t.shape[0]
    tt = min(tt, n)
    return pl.pallas_call(
        _router_kernel,
        grid=(B, n // tt),
        in_specs=[pl.BlockSpec((1, tt, D), lambda b, i: (b, i, 0)),
                  pl.BlockSpec((1, 1, D), lambda b, i: (b, 0, 0)),
                  pl.BlockSpec((1, 1, D), lambda b, i: (b, 0, 0)),
                  pl.BlockSpec((E, D), lambda b, i: (0, 0))],
        out_specs=pl.BlockSpec((1, E, tt), lambda b, i: (b, 0, i)),
        out_shape=jax.ShapeDtypeStruct((B, E, n), F32),
        compiler_params=_params(2, VMEM_MID),
        name="router",
    )(x, sc, sh, w_router_t)


def _exclusive_cumsum_lanes(mask, n):
    r = lax.broadcasted_iota(I32, (LANES, LANES), 0)
    c = lax.broadcasted_iota(I32, (LANES, LANES), 1)
    upper = jnp.where(r < c, 1.0, 0.0).astype(BF16)
    carry = jnp.zeros((mask.shape[0], 1), F32)
    out = []
    for j in range(n // LANES):
        blk = mask[:, j * LANES:(j + 1) * LANES]
        out.append(jnp.dot(blk.astype(BF16), upper, preferred_element_type=F32) + carry)
        carry = carry + jnp.sum(blk, -1, keepdims=True)
    return jnp.concatenate(out, axis=1)


def _topk_kernel(aff_ref, idx_ref, gate_ref, hi_ref, lo_ref, pieces_ref, *, n, cap, n_hi, n_hi_pad):
    E = aff_ref.shape[1]
    aff = aff_ref[0]
    bits = pltpu.bitcast(aff, I32)
    count = lambda m: jnp.sum(jnp.where(m, 1.0, 0.0), -1, keepdims=True)
    t = jnp.zeros((E, 1), I32)
    for bit in range(30, -1, -1):
        cand = t | (1 << bit)
        t = jnp.where(count(bits >= cand) >= cap, cand, t)
    gt = bits > t
    eq = bits == t
    need = cap - count(gt)
    eq_rank = _exclusive_cumsum_lanes(jnp.where(eq, 1.0, 0.0), n)
    sel = gt | (eq & (eq_rank < need))
    pos = _exclusive_cumsum_lanes(jnp.where(sel, 1.0, 0.0), n).astype(I32)
    hi_ref[...] = jnp.where(sel, pos >> int(math.log2(TOPK_LO)), -1)
    lo_ref[...] = pos & (TOPK_LO - 1)
    g_hi = aff.astype(BF16).astype(F32)
    g_mid = (aff - g_hi).astype(BF16).astype(F32)
    pieces_ref[0 * E:1 * E, :] = g_hi
    pieces_ref[1 * E:2 * E, :] = g_mid
    pieces_ref[2 * E:3 * E, :] = aff - g_hi - g_mid
    tok = lax.broadcasted_iota(I32, (1, n), 1)
    tok_hi = (tok >> 6).astype(F32)
    tok_lo = (tok & 63).astype(F32)
    hi_iota = lax.broadcasted_iota(I32, (n_hi_pad, n), 0)
    lo_iota = lax.broadcasted_iota(I32, (TOPK_LO, n), 0)

    def per_expert(e, carry):
        a = jnp.where(hi_ref[pl.ds(e, 1), :] == hi_iota, 1.0, 0.0).astype(BF16)
        b = jnp.where(lo_ref[pl.ds(e, 1), :] == lo_iota, 1.0, 0.0)
        vals = [tok_hi, tok_lo] + [pieces_ref[pl.ds(k * E + e, 1), :] for k in range(3)]
        rhs = jnp.concatenate([(b * v).astype(BF16) for v in vals], axis=0)
        res = lax.dot_general(a, rhs, _NT, preferred_element_type=F32)
        part = lambda k: res[:n_hi, k * TOPK_LO:(k + 1) * TOPK_LO]
        idx_ref[0, pl.ds(e, 1)] = (part(0) * 64.0 + part(1)).astype(I32)[None]
        gate_ref[0, pl.ds(e, 1)] = (part(2) + part(3) + part(4))[None]
        return carry

    lax.fori_loop(0, E, per_expert, 0, unroll=TOPK_ROW_UNROLL)


def topk_select(aff, cap):
    B, E, n = aff.shape
    R = B * E
    n_hi = cap // TOPK_LO
    n_hi_pad = -(-n_hi // SUBLANES) * SUBLANES
    idx, gate = pl.pallas_call(
        functools.partial(_topk_kernel, n=n, cap=cap, n_hi=n_hi, n_hi_pad=n_hi_pad),
        grid=(1,),
        in_specs=[pl.BlockSpec((1, R, n), lambda b: (0, 0, 0))],
        out_specs=[pl.BlockSpec((1, R, n_hi, TOPK_LO), lambda b: (0, 0, 0, 0)),
                   pl.BlockSpec((1, R, n_hi, TOPK_LO), lambda b: (0, 0, 0, 0))],
        out_shape=[jax.ShapeDtypeStruct((1, R, n_hi, TOPK_LO), I32), jax.ShapeDtypeStruct((1, R, n_hi, TOPK_LO), F32)],
        scratch_shapes=[pltpu.VMEM((R, n), I32), pltpu.VMEM((R, n), I32), pltpu.VMEM((3 * R, n), F32)],
        compiler_params=_params(1, VMEM_MID),
        name="topk_select",
    )(aff.reshape(1, R, n))
    return idx.reshape(B, E, cap), gate.reshape(B, E, cap)


def _moe_kernel(idx_ref, gate_ref, x_ref, sc_ref, sh_ref, og_ref, lng_ref, lnb_ref, wg_ref, wu_ref, wd_ref, y_ref,
                xs32_ref, xs_ref, ye_ref, yp_ref, *, S, nfc, alpha):
    e = pl.program_id(0)
    f = pl.program_id(1)
    E = pl.num_programs(0)
    per_step = S // nfc // SUBLANES

    def gather8(expert, base):
        for j in range(SUBLANES):
            i = idx_ref[expert * S + base + j]
            xs32_ref[pl.ds(base + j, 1), :] = x_ref[0, pl.ds(i, 1), :]

    def scatter8(expert, base, scale):
        rows, news = [], []
        for j in range(SUBLANES):
            i = idx_ref[expert * S + base + j]
            g = gate_ref[expert * S + base + j] * scale
            rows.append(pl.ds(i, 1))
            news.append(y_ref[0, pl.ds(i, 1), :] + g * yp_ref[pl.ds(base + j, 1), :])
        for r, new in zip(rows, news):
            y_ref[0, r, :] = new

    def modulated_bf16():
        return (xs32_ref[...] * (1.0 + sc_ref[0]) + sh_ref[0]).astype(BF16)

    @pl.when((e == 0) & (f == 0))
    def _():
        y_ref[0] = jnp.zeros(y_ref.shape[1:], F32)
        yp_ref[...] = jnp.zeros(yp_ref.shape, F32)

        def body(s8, carry):
            gather8(0, pl.multiple_of(s8 * SUBLANES, SUBLANES))
            return carry

        lax.fori_loop(0, S // SUBLANES, body, 0)
        xs_ref[...] = modulated_bf16()

    prev = jnp.maximum(e - 1, 0)
    prev_scale = jnp.where(e > 0, 1.0, 0.0)
    nxt = jnp.minimum(e + 1, E - 1)
    def row_ops(part_index, n_parts):
        for s8 in range(part_index, per_step, n_parts):
            base = pl.multiple_of((f * per_step + s8) * SUBLANES, SUBLANES)
            scatter8(prev, base, prev_scale)
            gather8(nxt, base)

    xs = xs_ref[...]
    row_ops(0, 3)
    a = jnp.dot(xs, wg_ref[0], preferred_element_type=F32)
    row_ops(1, 3)
    u = jnp.dot(xs, wu_ref[0], preferred_element_type=F32)
    row_ops(2, 3)
    part = jnp.dot((_silu(a) * u).astype(BF16), wd_ref[0], preferred_element_type=F32)

    @pl.when(f == 0)
    def _():
        ye_ref[...] = part

    @pl.when(f > 0)
    def _():
        ye_ref[...] += part

    @pl.when(f == nfc - 1)
    def _():
        yp_ref[...] = ye_ref[...]
        xs_ref[...] = modulated_bf16()

    @pl.when((e == E - 1) & (f == nfc - 1))
    def _():
        def body(s8, carry):
            scatter8(e, pl.multiple_of(s8 * SUBLANES, SUBLANES), 1.0)
            return carry

        lax.fori_loop(0, S // SUBLANES, body, 0)

        def ln_body(t, carry):
            rows = pl.ds(pl.multiple_of(t * MOE_LN_ROWS, MOE_LN_ROWS), MOE_LN_ROWS)
            z = alpha * x_ref[0, rows, :] + og_ref[0] * y_ref[0, rows, :]
            y_ref[0, rows, :] = _layernorm_rows(z, lng_ref[...], lnb_ref[...])
            return carry

        lax.fori_loop(0, y_ref.shape[1] // MOE_LN_ROWS, ln_body, 0)


def moe_experts(idx, gate, x, g, sc, sh, out_gate, ln_g, ln_b, wg, wu, wd, e0, n_experts, fc, alpha):
    G, R, D = x.shape
    F = wg.shape[2]
    E = n_experts
    S = idx.shape[0] // E
    nfc = F // fc
    grp = lambda e, f, *_: (g, 0, 0)
    vec = lambda e, f, *_: (0, 0)
    return pl.pallas_call(
        functools.partial(_moe_kernel, S=S, nfc=nfc, alpha=alpha),
        grid_spec=pltpu.PrefetchScalarGridSpec(
            num_scalar_prefetch=2,
            grid=(E, nfc),
            in_specs=[pl.BlockSpec((1, R, D), grp, pipeline_mode=pl.Buffered(1)),
                      pl.BlockSpec((1, 1, D), grp),
                      pl.BlockSpec((1, 1, D), grp),
                      pl.BlockSpec((1, 1, D), grp),
                      pl.BlockSpec((1, D), vec),
                      pl.BlockSpec((1, D), vec),
                      pl.BlockSpec((1, D, fc), lambda e, f, *_: (e0 + e, 0, f)),
                      pl.BlockSpec((1, D, fc), lambda e, f, *_: (e0 + e, 0, f)),
                      pl.BlockSpec((1, fc, D), lambda e, f, *_: (e0 + e, f, 0))],
            out_specs=pl.BlockSpec((1, R, D), grp, pipeline_mode=pl.Buffered(1)),
            scratch_shapes=[pltpu.VMEM((S, D), F32), pltpu.VMEM((S, D), BF16), pltpu.VMEM((S, D), F32),
                            pltpu.VMEM((S, D), F32)]),
        out_shape=jax.ShapeDtypeStruct((G, R, D), F32),
        input_output_aliases={2: 0},
        compiler_params=_params(2, VMEM_BIG),
        name="moe_experts",
    )(idx, gate, x, sc, sh, out_gate, ln_g.reshape(1, D), ln_b.reshape(1, D), wg, wu, wd)


def ec_moe_residual(x, sc, sh, out_gate, ln_g, ln_b, w_router_t, wg, wu, wd, e0, alpha, flatten_groups):
    B, n, D = x.shape
    E = w_router_t.shape[0]
    cap = EC_CAPACITY_FACTOR * n // E
    aff = router(x, sc, sh, w_router_t, 1024)
    idx, gate = topk_select(aff, cap)
    fc = min(MOE_HIDDEN_CHUNK, wg.shape[2])
    if flatten_groups:
        idx = idx + (jnp.arange(B, dtype=I32) * n)[:, None, None]
        idx = idx.transpose(1, 0, 2).reshape(-1)
        gate = gate.transpose(1, 0, 2).reshape(-1)
        x = moe_experts(idx, gate, x.reshape(1, B * n, D), 0, sc[:1], sh[:1], out_gate[:1], ln_g, ln_b,
                        wg, wu, wd, e0, E, fc, alpha)
        return x.reshape(B, n, D)
    for b in range(B):
        x = moe_experts(idx[b].reshape(-1), gate[b].reshape(-1), x, b, sc, sh, out_gate, ln_g, ln_b,
                        wg, wu, wd, e0, E, fc, alpha)
    return x


def kernel(x, c, ctx, c_ctx, ada_w, ada_b, ln_g, ln_b, na_w_qkv, na_w_o, na_rpb, gdn_w_in, gdn_conv_w,
           gdn_a_log, gdn_dt_bias, gdn_norm_w, gdn_w_o, moe_w_router, moe_w_gate, moe_w_up, moe_w_down):
    B, N, D = x.shape
    L = ctx.shape[1]
    depth = ada_w.shape[0]
    alpha = (2.0 * depth) ** 0.25
    xc = ctx

    rows = -(-(B + 1) // SUBLANES) * SUBLANES
    cs = jnp.zeros((rows, D), F32).at[:B].set(c).at[B].set(c_ctx)
    mods = ada_modulation(cs, ada_w, ada_b)

    E = moe_w_gate.shape[1]
    stack = lambda w: w.astype(BF16).reshape((depth * E,) + w.shape[2:])
    wg_all, wu_all, wd_all = stack(moe_w_gate), stack(moe_w_up), stack(moe_w_down)

    for l in range(depth):
        last = l == depth - 1
        i = l // 2
        mod = [mods[l, :B, j * D:(j + 1) * D].reshape(B, 1, D) for j in range(6)]
        modc = [jnp.broadcast_to(mods[l, B, j * D:(j + 1) * D].reshape(1, 1, D), (B, 1, D)) for j in range(6)]
        g1, b1, g2, b2 = ln_g[l, 0], ln_b[l, 0], ln_g[l, 1], ln_b[l, 1]
        if l % 2 == 0:
            w_qkv = na_w_qkv[i].astype(BF16)
            qkv = mm_mod(x, mod[1], mod[0], w_qkv, BF16, 512)
            qkvc = mm_mod(xc, modc[1], modc[0], w_qkv, BF16, 512)
            o = natten(qkv, qkvc, _natten_bias_table(na_rpb[i]))
            x = mm_ln(o, x, mod[2], na_w_o[i], g1, b1, alpha, 512)
            if not last:
                xc = mm_ln(ctx_attention(qkvc), xc, modc[2], na_w_o[i], g1, b1, alpha, 512)
        else:
            w_main = gdn_w_in[i][:, :4 * D].astype(BF16)
            w_ab = jnp.pad(gdn_w_in[i][:, 4 * D:], ((0, 0), (0, LANES - 4 * GDN_HEADS))).astype(BF16)

            def project(u, m):
                p = mm_mod(u, m[1], m[0], w_main, F32, 512)
                ab = mm_mod(u, m[1], m[0], w_ab, F32, 1024)
                gb = gdn_gates(ab, gdn_a_log[i], gdn_dt_bias[i], 1024)
                gbt = jnp.swapaxes(gb[:, :, :4 * GDN_HEADS], 1, 2)
                return p, gdn_conv(p, gdn_conv_w[i], 512), gb, gbt

            p, qkv, gb, gbt = project(x, mod)
            pc, qkvc, gbc, gbtc = project(xc, modc)
            s0 = jnp.zeros((B, 2, GDN_HEADS, GDN_HEAD_DIM, GDN_HEAD_DIM), F32)
            oc_f, oc_b, s_ctx = gdn_scan(qkvc, gbc, gbtc, s0)
            o_f, o_b, _ = gdn_scan(qkv, gb, gbt, s_ctx)
            x = mm_ln_gdn(o_f, o_b, p, gdn_norm_w[i], x, mod[2], gdn_w_o[i], g1, b1, alpha, 512)
            if not last:
                xc = mm_ln_gdn(oc_f, oc_b, pc, gdn_norm_w[i], xc, modc[2], gdn_w_o[i], g1, b1, alpha, 512)
        wr_t = moe_w_router[l].T
        x = ec_moe_residual(x, mod[4], mod[3], mod[5], g2, b2, wr_t, wg_all, wu_all, wd_all, l * E, alpha, False)
        if not last:
            xc = ec_moe_residual(xc, modc[4], modc[3], modc[5], g2, b2, wr_t, wg_all, wu_all, wd_all, l * E, alpha,
                                 True)
    return x
```

```python
import functools
import math

import numpy as np
import jax
import jax.numpy as jnp
from jax import lax
from jax.experimental import pallas as pl
from jax.experimental.pallas import tpu as pltpu

F32 = jnp.float32
BF16 = jnp.bfloat16
I32 = jnp.int32
HIGHEST = lax.Precision.HIGHEST

GRID_W = 64
NA_HEADS = 16
NA_HEAD_DIM = 64
NA_KR = 8
NA_KC = 16
GDN_HEADS = 8
GDN_HEAD_DIM = 128
GDN_CONV = 5
N_EXPERTS = 16
EC_CAPACITY_FACTOR = 2
LN_EPS = 1e-6
NEG_INF = -1e30

GDN_CHUNK = 128
CONV_HALO = 8
SUBLANES = 8
LANES = 128
MOE_HIDDEN_CHUNK = 1024
MOE_LN_ROWS = 256
NA_ROWS_PER_STEP = 2
TOPK_LO = 16
TOPK_ROW_UNROLL = 8
V7X_VMEM_BYTES = 64 * 1024 * 1024
VMEM_BIG = 56 * 1024 * 1024
VMEM_MID = 40 * 1024 * 1024

_NT = (((1,), (1,)), ((), ()))
_TN = (((0,), (0,)), ((), ()))


def _params(n_axes, vmem):
    return pltpu.CompilerParams(dimension_semantics=("arbitrary",) * n_axes, vmem_limit_bytes=vmem)


def _silu(x):
    return x * jax.nn.sigmoid(x)


def _layernorm_rows(z, g, b):
    mu = jnp.mean(z, -1, keepdims=True)
    zc = z - mu
    var = jnp.mean(zc * zc, -1, keepdims=True)
    return zc * lax.rsqrt(var + LN_EPS) * g + b


def _ada_kernel(cs_ref, w_ref, b_ref, o_ref):
    s = _silu(cs_ref[...])
    o_ref[0] = jnp.dot(s, w_ref[0], preferred_element_type=F32, precision=HIGHEST) + b_ref[0]


def ada_modulation(cs, ada_w, ada_b):
    depth, D, D6 = ada_w.shape
    R = cs.shape[0]
    tn = D6 // 4
    return pl.pallas_call(
        _ada_kernel,
        grid=(depth, D6 // tn),
        in_specs=[pl.BlockSpec((R, D), lambda l, j: (0, 0)),
                  pl.BlockSpec((1, D, tn), lambda l, j: (l, 0, j)),
                  pl.BlockSpec((1, 1, tn), lambda l, j: (l, 0, j))],
        out_specs=pl.BlockSpec((1, R, tn), lambda l, j: (l, 0, j)),
        out_shape=jax.ShapeDtypeStruct((depth, R, D6), F32),
        compiler_params=_params(2, VMEM_MID),
        name="ada_modulation",
    )(cs, ada_w, ada_b.reshape(depth, 1, D6))


def _mm_mod_kernel(x_ref, sc_ref, sh_ref, w_ref, o_ref):
    h = x_ref[0] * (1.0 + sc_ref[0]) + sh_ref[0]
    o_ref[0] = jnp.dot(h.astype(BF16), w_ref[...], preferred_element_type=F32).astype(o_ref.dtype)


def mm_mod(x, sc, sh, w, out_dtype, tm):
    B, N, D = x.shape
    NO = w.shape[1]
    tm = min(tm, N)
    return pl.pallas_call(
        _mm_mod_kernel,
        grid=(B, N // tm),
        in_specs=[pl.BlockSpec((1, tm, D), lambda b, i: (b, i, 0)),
                  pl.BlockSpec((1, 1, D), lambda b, i: (b, 0, 0)),
                  pl.BlockSpec((1, 1, D), lambda b, i: (b, 0, 0)),
                  pl.BlockSpec((D, NO), lambda b, i: (0, 0), pipeline_mode=pl.Buffered(1))],
        out_specs=pl.BlockSpec((1, tm, NO), lambda b, i: (b, i, 0)),
        out_shape=jax.ShapeDtypeStruct((B, N, NO), out_dtype),
        compiler_params=_params(2, VMEM_MID),
        name="mm_mod",
    )(x, sc, sh, w)


def _natten_bias_table(rpb):
    H = rpb.shape[0]
    qc = np.arange(GRID_W)
    kc = np.arange(GRID_W)
    col_start = np.clip(qc - NA_KC // 2, 0, GRID_W - NA_KC)
    valid = (kc[None, :] >= col_start[:, None]) & (kc[None, :] < col_start[:, None] + NA_KC)
    dc = np.clip(kc[None, :] - qc[:, None], -(NA_KC - 1), NA_KC - 1) + NA_KC - 1
    tab = rpb.astype(F32)[:, :, dc] + jnp.where(jnp.asarray(valid), 0.0, NEG_INF).astype(F32)[None, None]
    variants = [tab[:, d0:d0 + NA_KR].transpose(0, 2, 1, 3).reshape(H, GRID_W, NA_KR * GRID_W)
                for d0 in range(NA_KR)]
    return jnp.stack(variants)


def _softmax_pv(s_list, v_list):
    m = functools.reduce(jnp.maximum, [jnp.max(s, -1, keepdims=True) for s in s_list])
    ps = [jnp.exp(s - m) for s in s_list]
    den = functools.reduce(lambda a, b: a + b, [jnp.sum(p, -1, keepdims=True) for p in ps])
    o = functools.reduce(lambda a, b: a + b,
                         [jnp.dot(p.astype(BF16), v, preferred_element_type=F32) for p, v in zip(ps, v_list)])
    return o / den


def _natten_kernel(q_ref, k_ref, v_ref, kc_ref, vc_ref, *rest, rows, scale):
    bias_refs, o_ref = rest[:-1], rest[-1]
    for sub, bias_ref in enumerate(bias_refs):
        r = pl.program_id(1) * len(bias_refs) + sub
        _natten_row(r, pl.ds(sub * GRID_W, GRID_W), q_ref, k_ref, v_ref, kc_ref, vc_ref, bias_ref, o_ref,
                    rows=rows, scale=scale)


def _natten_row(r, qrows, q_ref, k_ref, v_ref, kc_ref, vc_ref, bias_ref, o_ref, *, rows, scale):
    rs = jnp.clip(r - NA_KR // 2, 0, rows - NA_KR)
    start = pl.multiple_of(rs * GRID_W, GRID_W)
    win = pl.ds(start, NA_KR * GRID_W)
    lane = lax.broadcasted_iota(I32, (GRID_W, LANES), 1)
    per_tile = LANES // NA_HEAD_DIM
    masks = [(lane >= half * NA_HEAD_DIM) & (lane < (half + 1) * NA_HEAD_DIM) for half in range(per_tile)]
    tiles = [slice(pair * LANES, (pair + 1) * LANES) for pair in range(NA_HEADS // per_tile)]
    scores = []
    for pair, sl in enumerate(tiles):
        q2 = q_ref[0, qrows, sl] * scale
        for half in range(per_tile):
            qh = jnp.where(masks[half], q2, jnp.zeros_like(q2))
            s_w = lax.dot_general(qh, k_ref[0, win, sl], _NT, preferred_element_type=F32)
            s_c = lax.dot_general(qh, kc_ref[0, :, sl], _NT, preferred_element_type=F32)
            scores.append((s_w + bias_ref[0, pair * per_tile + half], s_c))
    maxes = [jnp.maximum(jnp.max(s_w, -1, keepdims=True), jnp.max(s_c, -1, keepdims=True)) for s_w, s_c in scores]
    probs = [(jnp.exp(s_w - m), jnp.exp(s_c - m)) for (s_w, s_c), m in zip(scores, maxes)]
    dens = [jnp.sum(p_w, -1, keepdims=True) + jnp.sum(p_c, -1, keepdims=True) for p_w, p_c in probs]
    for pair, sl in enumerate(tiles):
        o = None
        for half in range(per_tile):
            h = pair * per_tile + half
            p_w, p_c = probs[h]
            pv = (jnp.dot(p_w.astype(BF16), v_ref[0, win, sl], preferred_element_type=F32)
                  + jnp.dot(p_c.astype(BF16), vc_ref[0, :, sl], preferred_element_type=F32)) / dens[h]
            o = pv if o is None else jnp.where(masks[half], pv, o)
        o_ref[0, qrows, sl] = o.astype(o_ref.dtype)


def natten(qkv, qkvc, bias_tab):
    B, N, D3 = qkv.shape
    D = D3 // 3
    L = qkvc.shape[1]
    rows = N // GRID_W
    assert rows >= NA_KR and N % GRID_W == 0
    scale = NA_HEAD_DIM ** -0.5
    assert math.frexp(scale)[0] == 0.5, "q is pre-scaled in bf16: the scale must be a power of two"

    nr = NA_ROWS_PER_STEP
    assert rows % nr == 0

    def bias_index(sub):
        def index(b, i):
            r = i * nr + sub
            rs = jnp.clip(r - NA_KR // 2, 0, rows - NA_KR)
            return (rs - r + NA_KR - 1, 0, 0, 0)
        return index

    bias_spec = lambda sub: pl.BlockSpec((1, NA_HEADS, GRID_W, NA_KR * GRID_W), bias_index(sub))
    return pl.pallas_call(
        functools.partial(_natten_kernel, rows=rows, scale=scale),
        grid=(B, rows // nr),
        in_specs=[pl.BlockSpec((1, nr * GRID_W, D), lambda b, i: (b, i, 0)),
                  pl.BlockSpec((1, N, D), lambda b, i: (b, 0, 1)),
                  pl.BlockSpec((1, N, D), lambda b, i: (b, 0, 2)),
                  pl.BlockSpec((1, L, D), lambda b, i: (b, 0, 1)),
                  pl.BlockSpec((1, L, D), lambda b, i: (b, 0, 2))] + [bias_spec(sub) for sub in range(nr)],
        out_specs=pl.BlockSpec((1, nr * GRID_W, D), lambda b, i: (b, i, 0)),
        out_shape=jax.ShapeDtypeStruct((B, N, D), BF16),
        compiler_params=_params(2, VMEM_BIG),
        name="natten",
    )(qkv, qkv, qkv, qkvc, qkvc, *([bias_tab] * nr))


def _ctx_attn_kernel(q_ref, k_ref, v_ref, o_ref, *, scale):
    for h in range(NA_HEADS):
        sl = slice(h * NA_HEAD_DIM, (h + 1) * NA_HEAD_DIM)
        s = lax.dot_general(q_ref[0, :, sl], k_ref[0, :, sl], _NT, preferred_element_type=F32) * scale
        o_ref[0, :, sl] = _softmax_pv([s], [v_ref[0, :, sl]]).astype(o_ref.dtype)


def ctx_attention(qkvc):
    B, L, D3 = qkvc.shape
    D = D3 // 3
    return pl.pallas_call(
        functools.partial(_ctx_attn_kernel, scale=NA_HEAD_DIM ** -0.5),
        grid=(B,),
        in_specs=[pl.BlockSpec((1, L, D), lambda b: (b, 0, 0)),
                  pl.BlockSpec((1, L, D), lambda b: (b, 0, 1)),
                  pl.BlockSpec((1, L, D), lambda b: (b, 0, 2))],
        out_specs=pl.BlockSpec((1, L, D), lambda b: (b, 0, 0)),
        out_shape=jax.ShapeDtypeStruct((B, L, D), BF16),
        compiler_params=_params(1, VMEM_MID),
        name="ctx_attention",
    )(qkvc, qkvc, qkvc)


def _is_first_step():
    return (pl.program_id(0) == 0) & (pl.program_id(1) == 0)


def _mm_ln_kernel(a_ref, x_ref, gate_ref, w_ref, g_ref, b_ref, o_ref, wb_ref, *, alpha):
    @pl.when(_is_first_step())
    def _():
        wb_ref[...] = w_ref[...].astype(BF16)

    y = jnp.dot(a_ref[0], wb_ref[...], preferred_element_type=F32)
    o_ref[0] = _layernorm_rows(alpha * x_ref[0] + gate_ref[0] * y, g_ref[...], b_ref[...])


def mm_ln(a, x, gate, w, g, b, alpha, tm):
    B, N, D = x.shape
    tm = min(tm, N)
    row = lambda bi, i: (bi, i, 0)
    vec = lambda bi, i: (0, 0)
    return pl.pallas_call(
        functools.partial(_mm_ln_kernel, alpha=alpha),
        grid=(B, N // tm),
        in_specs=[pl.BlockSpec((1, tm, D), row),
                  pl.BlockSpec((1, tm, D), row),
                  pl.BlockSpec((1, 1, D), lambda bi, i: (bi, 0, 0)),
                  pl.BlockSpec((D, D), vec),
                  pl.BlockSpec((1, D), vec),
                  pl.BlockSpec((1, D), vec)],
        out_specs=pl.BlockSpec((1, tm, D), row),
        out_shape=jax.ShapeDtypeStruct((B, N, D), F32),
        scratch_shapes=[pltpu.VMEM((D, D), BF16)],
        compiler_params=_params(2, VMEM_MID),
        name="mm_ln",
    )(a, x, gate, w, g.reshape(1, D), b.reshape(1, D))


def _mm_ln_gdn_kernel(of_ref, ob_ref, z_ref, nw_ref, x_ref, gate_ref, w_ref, g_ref, b_ref, o_ref,
                      wb_ref, a_ref, *, alpha):
    @pl.when(_is_first_step())
    def _():
        wb_ref[...] = w_ref[...].astype(BF16)

    for h in range(GDN_HEADS):
        sl = slice(h * GDN_HEAD_DIM, (h + 1) * GDN_HEAD_DIM)
        o = of_ref[0, :, sl] + ob_ref[0, :, sl]
        y = o * lax.rsqrt(jnp.mean(o * o, -1, keepdims=True) + LN_EPS) * nw_ref[...]
        a_ref[:, sl] = (y * _silu(z_ref[0, :, sl])).astype(BF16)
    y = jnp.dot(a_ref[...], wb_ref[...], preferred_element_type=F32)
    o_ref[0] = _layernorm_rows(alpha * x_ref[0] + gate_ref[0] * y, g_ref[...], b_ref[...])


def mm_ln_gdn(o_f, o_b, p, norm_w, x, gate, w, g, b, alpha, tm):
    B, N, D = x.shape
    tm = min(tm, N)
    row = lambda bi, i: (bi, i, 0)
    vec = lambda bi, i: (0, 0)
    return pl.pallas_call(
        functools.partial(_mm_ln_gdn_kernel, alpha=alpha),
        grid=(B, N // tm),
        in_specs=[pl.BlockSpec((1, tm, D), row),
                  pl.BlockSpec((1, tm, D), row),
                  pl.BlockSpec((1, tm, D), lambda bi, i: (bi, i, 3)),
                  pl.BlockSpec((1, GDN_HEAD_DIM), vec),
                  pl.BlockSpec((1, tm, D), row),
                  pl.BlockSpec((1, 1, D), lambda bi, i: (bi, 0, 0)),
                  pl.BlockSpec((D, D), vec),
                  pl.BlockSpec((1, D), vec),
                  pl.BlockSpec((1, D), vec)],
        out_specs=pl.BlockSpec((1, tm, D), row),
        out_shape=jax.ShapeDtypeStruct((B, N, D), F32),
        scratch_shapes=[pltpu.VMEM((D, D), BF16), pltpu.VMEM((tm, D), BF16)],
        compiler_params=_params(2, VMEM_MID),
        name="mm_ln_gdn",
    )(o_f, o_b, p, norm_w.reshape(1, GDN_HEAD_DIM), x, gate, w, g.reshape(1, D), b.reshape(1, D))


def _conv_kernel(prev_ref, main_ref, next_ref, w_ref, o_ref, ext_ref, *, tt, nt):
    i = pl.program_id(1)
    j = pl.program_id(2)
    ext_ref[0:CONV_HALO, :] = jnp.where(i > 0, prev_ref[0], 0.0)
    ext_ref[CONV_HALO:CONV_HALO + tt, :] = main_ref[0]
    ext_ref[CONV_HALO + tt:, :] = jnp.where(i < nt - 1, next_ref[0], 0.0)
    pad = GDN_CONV // 2
    ext = ext_ref[...]
    rows = ext.shape[0]
    acc = None
    for t in range(GDN_CONV):
        shifted = ext if t == pad else pltpu.roll(ext, (pad - t) % rows, 0)
        term = w_ref[t:t + 1, :] * shifted[CONV_HALO:CONV_HALO + tt, :]
        acc = term if acc is None else acc + term
    y = _silu(acc)
    qscale = jnp.where(j == 0, GDN_HEAD_DIM ** -0.5, 1.0)
    for h in range(GDN_HEADS):
        sl = slice(h * GDN_HEAD_DIM, (h + 1) * GDN_HEAD_DIM)
        seg = y[:, sl]
        inv = lax.rsqrt(jnp.sum(seg * seg, -1, keepdims=True) + 1e-6) * qscale
        o_ref[0, :, sl] = seg * jnp.where(j < 2, inv, 1.0)


def gdn_conv(p, conv_w, tt):
    B, T = p.shape[:2]
    D = conv_w.shape[1] // 3
    tt = min(tt, T)
    nt = T // tt
    hb = tt // CONV_HALO
    return pl.pallas_call(
        functools.partial(_conv_kernel, tt=tt, nt=nt),
        grid=(B, nt, 3),
        in_specs=[pl.BlockSpec((1, CONV_HALO, D), lambda b, i, j: (b, jnp.maximum(i * hb - 1, 0), j)),
                  pl.BlockSpec((1, tt, D), lambda b, i, j: (b, i, j)),
                  pl.BlockSpec((1, CONV_HALO, D), lambda b, i, j: (b, jnp.minimum((i + 1) * hb, T // CONV_HALO - 1), j)),
                  pl.BlockSpec((GDN_CONV, D), lambda b, i, j: (0, j))],
        out_specs=pl.BlockSpec((1, tt, D), lambda b, i, j: (b, i, j)),
        out_shape=jax.ShapeDtypeStruct((B, T, 3 * D), F32),
        scratch_shapes=[pltpu.VMEM((tt + 2 * CONV_HALO, D), F32)],
        compiler_params=_params(3, VMEM_MID),
        name="gdn_conv",
    )(p, p, p, conv_w)


def _gates_kernel(ab_ref, alog_ref, dtb_ref, o_ref):
    ab = ab_ref[0]
    xg = ab + dtb_ref[...]
    softplus = jnp.maximum(xg, 0.0) + jnp.log(1.0 + jnp.exp(-jnp.abs(xg)))
    g = -jnp.exp(alog_ref[...]) * softplus
    beta = jax.nn.sigmoid(ab)
    lane = lax.broadcasted_iota(I32, ab.shape, 1)
    o_ref[0] = jnp.where(lane < 2 * GDN_HEADS, g, beta)


def gdn_gates(ab, a_log, dt_bias, tt):
    B, T, W = ab.shape
    tt = min(tt, T)
    pad = lambda a: jnp.pad(a.reshape(1, -1).astype(F32), ((0, 0), (0, W - a.size)))
    return pl.pallas_call(
        _gates_kernel,
        grid=(B, T // tt),
        in_specs=[pl.BlockSpec((1, tt, W), lambda b, i: (b, i, 0)),
                  pl.BlockSpec((1, W), lambda b, i: (0, 0)),
                  pl.BlockSpec((1, W), lambda b, i: (0, 0))],
        out_specs=pl.BlockSpec((1, tt, W), lambda b, i: (b, i, 0)),
        out_shape=jax.ShapeDtypeStruct((B, T, W), F32),
        compiler_params=_params(2, VMEM_MID),
        name="gdn_gates",
    )(ab, pad(a_log), pad(dt_bias))


TRI_BASE = 16


def _bmm(a, b):
    return jnp.einsum("hij,hjk->hik", a.astype(BF16), b.astype(BF16), preferred_element_type=F32)


def _unit_triangular_inverse(A, eye):
    C = A.shape[-1]
    r = lax.broadcasted_iota(I32, (C, C), 0)
    c = lax.broadcasted_iota(I32, (C, C), 1)
    sh = int(math.log2(TRI_BASE))
    B = jnp.where((r >> sh) == (c >> sh), A, 0.0)
    T = eye - B
    P = B
    for _ in range(sh - 1):
        P = _bmm(P, P)
        T = T + _bmm(T, P)
    while sh < int(math.log2(C)):
        off = ((r >> (sh + 1)) == (c >> (sh + 1))) & ((r >> sh) != (c >> sh))
        T = T - _bmm(_bmm(T, jnp.where(off, A, 0.0)), T)
        sh += 1
    return T


def _gdn_stream_operands(d, q_ref, k_ref, v_ref, gb, Gc_all, Gr_all, incl, strict):
    C, H, dk = GDN_CHUNK, GDN_HEADS, GDN_HEAD_DIM
    last = C - 1 if d == 0 else 0
    cols = [d * H + h for h in range(H)]
    heads = lambda f: jnp.stack([f(h, cols[h]) for h in range(H)])
    Gc = heads(lambda h, col: Gc_all[:, col:col + 1])
    Gr = heads(lambda h, col: Gr_all[col:col + 1, :])
    decay = jnp.where(incl, jnp.exp(jnp.minimum(Gc - Gr, 0.0)), 0.0)
    return dict(
        Gc=Gc,
        Gtot=heads(lambda h, col: Gc_all[last:last + 1, col:col + 1]),
        beta=heads(lambda h, col: gb[:, 2 * H + col:2 * H + col + 1]),
        q=heads(lambda h, col: q_ref[0, :, h * dk:(h + 1) * dk]),
        k=heads(lambda h, col: k_ref[0, :, h * dk:(h + 1) * dk]),
        v=heads(lambda h, col: v_ref[0, :, h * dk:(h + 1) * dk]),
        decay=decay,
        decay_strict=jnp.where(strict, decay, 0.0))


def _gdn_chunk(ops, S, eye):
    C, dk = GDN_CHUNK, GDN_HEAD_DIM
    q, k, v, beta, Gc, Gtot = (ops[name] for name in ("q", "k", "v", "beta", "Gc", "Gtot"))
    kb = k * beta
    kq = jnp.einsum("hid,hjd->hij", jnp.concatenate([kb, q], axis=1).astype(BF16), k.astype(BF16),
                    preferred_element_type=F32)
    A = kq[:, :C] * ops["decay_strict"]
    attn = kq[:, C:] * ops["decay"]
    T = _unit_triangular_inverse(A, eye)
    eG = jnp.exp(Gc)
    wu = _bmm(T, jnp.concatenate([kb * eG, v * beta], axis=2))
    wq = _bmm(jnp.concatenate([wu[:, :, :dk], q * eG], axis=1), S)
    v_new = wu[:, :, dk:] - wq[:, :C]
    o = wq[:, C:] + _bmm(attn, v_new)
    k_tail = k * jnp.exp(Gtot - Gc)
    S_new = S * jnp.exp(Gtot) + jnp.einsum("hcd,hce->hde", k_tail.astype(BF16), v_new.astype(BF16),
                                           preferred_element_type=F32)
    return o, S_new


def _gdn_scan_kernel(qf_ref, kf_ref, vf_ref, gbf_ref, gtf_ref, qb_ref, kb_ref, vb_ref, gbb_ref, gtb_ref,
                     s0_ref, of_ref, ob_ref, sout_ref, S_ref, *, nsteps):
    i = pl.program_id(1)

    @pl.when(i == 0)
    def _():
        S_ref[...] = s0_ref[0]

    C = GDN_CHUNK
    r = lax.broadcasted_iota(I32, (C, C), 0)
    c = lax.broadcasted_iota(I32, (C, C), 1)
    eye = jnp.where(r == c, 1.0, 0.0).astype(F32)
    streams = ((qf_ref, kf_ref, vf_ref, gbf_ref, gtf_ref, of_ref, r >= c, r > c),
               (qb_ref, kb_ref, vb_ref, gbb_ref, gtb_ref, ob_ref, r <= c, r < c))
    per_stream = []
    for d, (q_ref, k_ref, v_ref, gb_ref, gt_ref, o_ref, incl, strict) in enumerate(streams):
        tri = jnp.where(incl, 1.0, 0.0).astype(F32)
        gb = gb_ref[0]
        Gc_all = jnp.dot(tri, gb, preferred_element_type=F32, precision=HIGHEST)
        Gr_all = lax.dot_general(gt_ref[0], tri, _NT, preferred_element_type=F32, precision=HIGHEST)
        per_stream.append(_gdn_stream_operands(d, q_ref, k_ref, v_ref, gb, Gc_all, Gr_all, incl, strict))
    ops = {name: jnp.concatenate([p[name] for p in per_stream], axis=0) for name in per_stream[0]}
    H, dk = GDN_HEADS, GDN_HEAD_DIM
    o, S_new = _gdn_chunk(ops, S_ref[...].reshape(2 * H, dk, dk), eye)
    S_ref[...] = S_new.reshape(2, H, dk, dk)
    for d, stream in enumerate(streams):
        o_ref = stream[5]
        for h in range(H):
            o_ref[0, :, h * dk:(h + 1) * dk] = o[d * H + h]

    @pl.when(i == nsteps - 1)
    def _():
        sout_ref[0] = S_ref[...]


def gdn_scan(qkv, gb, gbt, s0):
    B, T, D3 = qkv.shape
    D = D3 // 3
    C = GDN_CHUNK
    n = T // C
    W = gb.shape[2]
    R = gbt.shape[1]
    fwd = lambda j: (lambda b, i: (b, i, j))
    bwd = lambda j: (lambda b, i: (b, n - 1 - i, j))
    st = lambda b, i: (b, 0, 0, 0, 0)
    sspec = pl.BlockSpec((1, 2, GDN_HEADS, GDN_HEAD_DIM, GDN_HEAD_DIM), st)
    return pl.pallas_call(
        functools.partial(_gdn_scan_kernel, nsteps=n),
        grid=(B, n),
        in_specs=[pl.BlockSpec((1, C, D), fwd(0)), pl.BlockSpec((1, C, D), fwd(1)), pl.BlockSpec((1, C, D), fwd(2)),
                  pl.BlockSpec((1, C, W), fwd(0)), pl.BlockSpec((1, R, C), lambda b, i: (b, 0, i)),
                  pl.BlockSpec((1, C, D), bwd(0)), pl.BlockSpec((1, C, D), bwd(1)), pl.BlockSpec((1, C, D), bwd(2)),
                  pl.BlockSpec((1, C, W), bwd(0)), pl.BlockSpec((1, R, C), lambda b, i: (b, 0, n - 1 - i)),
                  sspec],
        out_specs=[pl.BlockSpec((1, C, D), fwd(0)), pl.BlockSpec((1, C, D), bwd(0)), sspec],
        out_shape=[jax.ShapeDtypeStruct((B, T, D), F32), jax.ShapeDtypeStruct((B, T, D), F32),
                   jax.ShapeDtypeStruct(s0.shape, F32)],
        scratch_shapes=[pltpu.VMEM((2, GDN_HEADS, GDN_HEAD_DIM, GDN_HEAD_DIM), F32)],
        compiler_params=_params(2, VMEM_MID),
        name="gdn_scan",
    )(qkv, qkv, qkv, gb, gbt, qkv, qkv, qkv, gb, gbt, s0)


def _router_kernel(x_ref, sc_ref, sh_ref, wr_ref, o_ref):
    h = x_ref[0] * (1.0 + sc_ref[0]) + sh_ref[0]
    w = wr_ref[...]
    w_hi = w.astype(BF16)
    w_lo = (w - w_hi.astype(F32)).astype(BF16)
    h_hi = h.astype(BF16)
    h_lo = (h - h_hi.astype(F32)).astype(BF16)
    nt = lambda a, b: lax.dot_general(a, b, _NT, preferred_element_type=F32)
    lg = nt(w_hi, h_hi) + (nt(w_hi, h_lo) + nt(w_lo, h_hi))
    e = jnp.exp(lg - jnp.max(lg, 0, keepdims=True))
    o_ref[0] = e / jnp.sum(e, 0, keepdims=True)


def router(x, sc, sh, w_router_t, tt):
    B, n, D = x.shape
    E = w_router_t.shape[0]
    tt = min(tt, n)
    return pl.pallas_call(
        _router_kernel,
        grid=(B, n // tt),
        in_specs=[pl.BlockSpec((1, tt, D), lambda b, i: (b, i, 0)),
                  pl.BlockSpec((1, 1, D), lambda b, i: (b, 0, 0)),
                  pl.BlockSpec((1, 1, D), lambda b, i: (b, 0, 0)),
                  pl.BlockSpec((E, D), lambda b, i: (0, 0))],
        out_specs=pl.BlockSpec((1, E, tt), lambda b, i: (b, 0, i)),
        out_shape=jax.ShapeDtypeStruct((B, E, n), F32),
        compiler_params=_params(2, VMEM_MID),
        name="router",
    )(x, sc, sh, w_router_t)


def _exclusive_cumsum_lanes(mask, n):
    r = lax.broadcasted_iota(I32, (LANES, LANES), 0)
    c = lax.broadcasted_iota(I32, (LANES, LANES), 1)
    upper = jnp.where(r < c, 1.0, 0.0).astype(BF16)
    carry = jnp.zeros((mask.shape[0], 1), F32)
    out = []
    for j in range(n // LANES):
        blk = mask[:, j * LANES:(j + 1) * LANES]
        out.append(jnp.dot(blk.astype(BF16), upper, preferred_element_type=F32) + carry)
        carry = carry + jnp.sum(blk, -1, keepdims=True)
    return jnp.concatenate(out, axis=1)


def _topk_kernel(aff_ref, idx_ref, gate_ref, hi_ref, lo_ref, pieces_ref, *, n, cap, n_hi, n_hi_pad):
    E = aff_ref.shape[1]
    aff = aff_ref[0]
    bits = pltpu.bitcast(aff, I32)
    count = lambda m: jnp.sum(jnp.where(m, 1.0, 0.0), -1, keepdims=True)
    t = jnp.zeros((E, 1), I32)
    for bit in range(30, -1, -1):
        cand = t | (1 << bit)
        t = jnp.where(count(bits >= cand) >= cap, cand, t)
    gt = bits > t
    eq = bits == t
    need = cap - count(gt)
    eq_rank = _exclusive_cumsum_lanes(jnp.where(eq, 1.0, 0.0), n)
    sel = gt | (eq & (eq_rank < need))
    pos = _exclusive_cumsum_lanes(jnp.where(sel, 1.0, 0.0), n).astype(I32)
    hi_ref[...] = jnp.where(sel, pos >> int(math.log2(TOPK_LO)), -1)
    lo_ref[...] = pos & (TOPK_LO - 1)
    g_hi = aff.astype(BF16).astype(F32)
    g_mid = (aff - g_hi).astype(BF16).astype(F32)
    pieces_ref[0 * E:1 * E, :] = g_hi
    pieces_ref[1 * E:2 * E, :] = g_mid
    pieces_ref[2 * E:3 * E, :] = aff - g_hi - g_mid
    tok = lax.broadcasted_iota(I32, (1, n), 1)
    tok_hi = (tok >> 6).astype(F32)
    tok_lo = (tok & 63).astype(F32)
    hi_iota = lax.broadcasted_iota(I32, (n_hi_pad, n), 0)
    lo_iota = lax.broadcasted_iota(I32, (TOPK_LO, n), 0)

    def per_expert(e, carry):
        a = jnp.where(hi_ref[pl.ds(e, 1), :] == hi_iota, 1.0, 0.0).astype(BF16)
        b = jnp.where(lo_ref[pl.ds(e, 1), :] == lo_iota, 1.0, 0.0)
        vals = [tok_hi, tok_lo] + [pieces_ref[pl.ds(k * E + e, 1), :] for k in range(3)]
        rhs = jnp.concatenate([(b * v).astype(BF16) for v in vals], axis=0)
        res = lax.dot_general(a, rhs, _NT, preferred_element_type=F32)
        part = lambda k: res[:n_hi, k * TOPK_LO:(k + 1) * TOPK_LO]
        idx_ref[0, pl.ds(e, 1)] = (part(0) * 64.0 + part(1)).astype(I32)[None]
        gate_ref[0, pl.ds(e, 1)] = (part(2) + part(3) + part(4))[None]
        return carry

    lax.fori_loop(0, E, per_expert, 0, unroll=TOPK_ROW_UNROLL)


def topk_select(aff, cap):
    B, E, n = aff.shape
    R = B * E
    n_hi = cap // TOPK_LO
    n_hi_pad = -(-n_hi // SUBLANES) * SUBLANES
    idx, gate = pl.pallas_call(
        functools.partial(_topk_kernel, n=n, cap=cap, n_hi=n_hi, n_hi_pad=n_hi_pad),
        grid=(1,),
        in_specs=[pl.BlockSpec((1, R, n), lambda b: (0, 0, 0))],
        out_specs=[pl.BlockSpec((1, R, n_hi, TOPK_LO), lambda b: (0, 0, 0, 0)),
                   pl.BlockSpec((1, R, n_hi, TOPK_LO), lambda b: (0, 0, 0, 0))],
        out_shape=[jax.ShapeDtypeStruct((1, R, n_hi, TOPK_LO), I32), jax.ShapeDtypeStruct((1, R, n_hi, TOPK_LO), F32)],
        scratch_shapes=[pltpu.VMEM((R, n), I32), pltpu.VMEM((R, n), I32), pltpu.VMEM((3 * R, n), F32)],
        compiler_params=_params(1, VMEM_MID),
        name="topk_select",
    )(aff.reshape(1, R, n))
    return idx.reshape(B, E, cap), gate.reshape(B, E, cap)


def _moe_kernel(idx_ref, gate_ref, x_ref, sc_ref, sh_ref, og_ref, lng_ref, lnb_ref, wg_ref, wu_ref, wd_ref, y_ref,
                xs32_ref, xs_ref, ye_ref, yp_ref, *, S, nfc, alpha):
    e = pl.program_id(0)
    f = pl.program_id(1)
    E = pl.num_programs(0)
    per_step = S // nfc // SUBLANES

    def gather8(expert, base):
        for j in range(SUBLANES):
            i = idx_ref[expert * S + base + j]
            xs32_ref[pl.ds(base + j, 1), :] = x_ref[0, pl.ds(i, 1), :]

    def scatter8(expert, base, scale):
        rows, news = [], []
        for j in range(SUBLANES):
            i = idx_ref[expert * S + base + j]
            g = gate_ref[expert * S + base + j] * scale
            rows.append(pl.ds(i, 1))
            news.append(y_ref[0, pl.ds(i, 1), :] + g * yp_ref[pl.ds(base + j, 1), :])
        for r, new in zip(rows, news):
            y_ref[0, r, :] = new

    def modulated_bf16():
        return (xs32_ref[...] * (1.0 + sc_ref[0]) + sh_ref[0]).astype(BF16)

    @pl.when((e == 0) & (f == 0))
    def _():
        y_ref[0] = jnp.zeros(y_ref.shape[1:], F32)
        yp_ref[...] = jnp.zeros(yp_ref.shape, F32)

        def body(s8, carry):
            gather8(0, pl.multiple_of(s8 * SUBLANES, SUBLANES))
            return carry

        lax.fori_loop(0, S // SUBLANES, body, 0)
        xs_ref[...] = modulated_bf16()

    prev = jnp.maximum(e - 1, 0)
    prev_scale = jnp.where(e > 0, 1.0, 0.0)
    nxt = jnp.minimum(e + 1, E - 1)
    def row_ops(part_index, n_parts):
        for s8 in range(part_index, per_step, n_parts):
            base = pl.multiple_of((f * per_step + s8) * SUBLANES, SUBLANES)
            scatter8(prev, base, prev_scale)
            gather8(nxt, base)

    xs = xs_ref[...]
    row_ops(0, 3)
    a = jnp.dot(xs, wg_ref[0], preferred_element_type=F32)
    row_ops(1, 3)
    u = jnp.dot(xs, wu_ref[0], preferred_element_type=F32)
    row_ops(2, 3)
    part = jnp.dot((_silu(a) * u).astype(BF16), wd_ref[0], preferred_element_type=F32)

    @pl.when(f == 0)
    def _():
        ye_ref[...] = part

    @pl.when(f > 0)
    def _():
        ye_ref[...] += part

    @pl.when(f == nfc - 1)
    def _():
        yp_ref[...] = ye_ref[...]
        xs_ref[...] = modulated_bf16()

    @pl.when((e == E - 1) & (f == nfc - 1))
    def _():
        def body(s8, carry):
            scatter8(e, pl.multiple_of(s8 * SUBLANES, SUBLANES), 1.0)
            return carry

        lax.fori_loop(0, S // SUBLANES, body, 0)

        def ln_body(t, carry):
            rows = pl.ds(pl.multiple_of(t * MOE_LN_ROWS, MOE_LN_ROWS), MOE_LN_ROWS)
            z = alpha * x_ref[0, rows, :] + og_ref[0] * y_ref[0, rows, :]
            y_ref[0, rows, :] = _layernorm_rows(z, lng_ref[...], lnb_ref[...])
            return carry

        lax.fori_loop(0, y_ref.shape[1] // MOE_LN_ROWS, ln_body, 0)


def moe_experts(idx, gate, x, g, sc, sh, out_gate, ln_g, ln_b, wg, wu, wd, e0, n_experts, fc, alpha):
    G, R, D = x.shape
    F = wg.shape[2]
    E = n_experts
    S = idx.shape[0] // E
    nfc = F // fc
    grp = lambda e, f, *_: (g, 0, 0)
    vec = lambda e, f, *_: (0, 0)
    return pl.pallas_call(
        functools.partial(_moe_kernel, S=S, nfc=nfc, alpha=alpha),
        grid_spec=pltpu.PrefetchScalarGridSpec(
            num_scalar_prefetch=2,
            grid=(E, nfc),
            in_specs=[pl.BlockSpec((1, R, D), grp, pipeline_mode=pl.Buffered(1)),
                      pl.BlockSpec((1, 1, D), grp),
                      pl.BlockSpec((1, 1, D), grp),
                      pl.BlockSpec((1, 1, D), grp),
                      pl.BlockSpec((1, D), vec),
                      pl.BlockSpec((1, D), vec),
                      pl.BlockSpec((1, D, fc), lambda e, f, *_: (e0 + e, 0, f)),
                      pl.BlockSpec((1, D, fc), lambda e, f, *_: (e0 + e, 0, f)),
                      pl.BlockSpec((1, fc, D), lambda e, f, *_: (e0 + e, f, 0))],
            out_specs=pl.BlockSpec((1, R, D), grp, pipeline_mode=pl.Buffered(1)),
            scratch_shapes=[pltpu.VMEM((S, D), F32), pltpu.VMEM((S, D), BF16), pltpu.VMEM((S, D), F32),
                            pltpu.VMEM((S, D), F32)]),
        out_shape=jax.ShapeDtypeStruct((G, R, D), F32),
        input_output_aliases={2: 0},
        compiler_params=_params(2, VMEM_BIG),
        name="moe_experts",
    )(idx, gate, x, sc, sh, out_gate, ln_g.reshape(1, D), ln_b.reshape(1, D), wg, wu, wd)


def ec_moe_residual(x, sc, sh, out_gate, ln_g, ln_b, w_router_t, wg, wu, wd, e0, alpha, flatten_groups):
    B, n, D = x.shape
    E = w_router_t.shape[0]
    cap = EC_CAPACITY_FACTOR * n // E
    aff = router(x, sc, sh, w_router_t, 1024)
    idx, gate = topk_select(aff, cap)
    fc = min(MOE_HIDDEN_CHUNK, wg.shape[2])
    if flatten_groups:
        idx = idx + (jnp.arange(B, dtype=I32) * n)[:, None, None]
        idx = idx.transpose(1, 0, 2).reshape(-1)
        gate = gate.transpose(1, 0, 2).reshape(-1)
        x = moe_experts(idx, gate, x.reshape(1, B * n, D), 0, sc[:1], sh[:1], out_gate[:1], ln_g, ln_b,
                        wg, wu, wd, e0, E, fc, alpha)
        return x.reshape(B, n, D)
    for b in range(B):
        x = moe_experts(idx[b].reshape(-1), gate[b].reshape(-1), x, b, sc, sh, out_gate, ln_g, ln_b,
                        wg, wu, wd, e0, E, fc, alpha)
    return x


def kernel(x, c, ctx, c_ctx, ada_w, ada_b, ln_g, ln_b, na_w_qkv, na_w_o, na_rpb, gdn_w_in, gdn_conv_w,
           gdn_a_log, gdn_dt_bias, gdn_norm_w, gdn_w_o, moe_w_router, moe_w_gate, moe_w_up, moe_w_down):
    B, N, D = x.shape
    L = ctx.shape[1]
    depth = ada_w.shape[0]
    alpha = (2.0 * depth) ** 0.25
    xc = ctx

    rows = -(-(B + 1) // SUBLANES) * SUBLANES
    cs = jnp.zeros((rows, D), F32).at[:B].set(c).at[B].set(c_ctx)
    mods = ada_modulation(cs, ada_w, ada_b)

    E = moe_w_gate.shape[1]
    stack = lambda w: w.astype(BF16).reshape((depth * E,) + w.shape[2:])
    wg_all, wu_all, wd_all = stack(moe_w_gate), stack(moe_w_up), stack(moe_w_down)

    for l in range(depth):
        last = l == depth - 1
        i = l // 2
        mod = [mods[l, :B, j * D:(j + 1) * D].reshape(B, 1, D) for j in range(6)]
        modc = [jnp.broadcast_to(mods[l, B, j * D:(j + 1) * D].reshape(1, 1, D), (B, 1, D)) for j in range(6)]
        g1, b1, g2, b2 = ln_g[l, 0], ln_b[l, 0], ln_g[l, 1], ln_b[l, 1]
        if l % 2 == 0:
            w_qkv = na_w_qkv[i].astype(BF16)
            qkv = mm_mod(x, mod[1], mod[0], w_qkv, BF16, 512)
            qkvc = mm_mod(xc, modc[1], modc[0], w_qkv, BF16, 512)
            o = natten(qkv, qkvc, _natten_bias_table(na_rpb[i]))
            x = mm_ln(o, x, mod[2], na_w_o[i], g1, b1, alpha, 512)
            if not last:
                xc = mm_ln(ctx_attention(qkvc), xc, modc[2], na_w_o[i], g1, b1, alpha, 512)
        else:
            w_main = gdn_w_in[i][:, :4 * D].astype(BF16)
            w_ab = jnp.pad(gdn_w_in[i][:, 4 * D:], ((0, 0), (0, LANES - 4 * GDN_HEADS))).astype(BF16)

            def project(u, m):
                p = mm_mod(u, m[1], m[0], w_main, F32, 512)
                ab = mm_mod(u, m[1], m[0], w_ab, F32, 1024)
                gb = gdn_gates(ab, gdn_a_log[i], gdn_dt_bias[i], 1024)
                gbt = jnp.swapaxes(gb[:, :, :4 * GDN_HEADS], 1, 2)
                return p, gdn_conv(p, gdn_conv_w[i], 512), gb, gbt

            p, qkv, gb, gbt = project(x, mod)
            pc, qkvc, gbc, gbtc = project(xc, modc)
            s0 = jnp.zeros((B, 2, GDN_HEADS, GDN_HEAD_DIM, GDN_HEAD_DIM), F32)
            oc_f, oc_b, s_ctx = gdn_scan(qkvc, gbc, gbtc, s0)
            o_f, o_b, _ = gdn_scan(qkv, gb, gbt, s_ctx)
            x = mm_ln_gdn(o_f, o_b, p, gdn_norm_w[i], x, mod[2], gdn_w_o[i], g1, b1, alpha, 512)
            if not last:
                xc = mm_ln_gdn(oc_f, oc_b, pc, gdn_norm_w[i], xc, modc[2], gdn_w_o[i], g1, b1, alpha, 512)
        wr_t = moe_w_router[l].T
        x = ec_moe_residual(x, mod[4], mod[3], mod[5], g2, b2, wr_t, wg_all, wu_all, wd_all, l * E, alpha, False)
        if not last:
            xc = ec_moe_residual(xc, modc[4], modc[3], modc[5], g2, b2, wr_t, wg_all, wu_all, wd_all, l * E, alpha,
                                 True)
    return x
```
